```python
import jax, jax.numpy as jnp
from jax import lax
import numpy as np

D_MODEL = 1024
BATCH = 16
SEQ = 2048
DEPTH = 1

N_MEM = 256
GRID_W = 64
HEAD_DIM = 64
N_Q_HEADS = D_MODEL // 128
N_KV_HEADS = N_Q_HEADS // 4
Q_PER_KV = N_Q_HEADS // N_KV_HEADS
ATTN_WIDTH = N_Q_HEADS * HEAD_DIM
KV_WIDTH = N_KV_HEADS * HEAD_DIM
POOL_WINDOWS = (2, 4, 8, 16)
N_POOL_GROUPS = len(POOL_WINDOWS)
POOL_WIDTH = D_MODEL // 2
POOL_GROUP = POOL_WIDTH // N_POOL_GROUPS
N_BRANCHES = 2
IN_WIDTH = ATTN_WIDTH + 2 * KV_WIDTH + POOL_WIDTH + N_BRANCHES * D_MODEL
Q_BLOCK = 128
ROPE_THETA = 10000.0
ROPE_AXIS_DIM = HEAD_DIM // 2
ROPE_FREQS = ROPE_AXIS_DIM // 2
N_X_HEADS = 4
X_HEAD_DIM = D_MODEL // N_X_HEADS
D_FF = 4 * D_MODEL
EPS = 1e-6

kernel_name = "hybrid_gqa_pool_memory_encoder"


def rmsnorm(x, g):
    xf = x.astype(jnp.float32)
    y = xf * lax.rsqrt(jnp.mean(xf * xf, axis=-1, keepdims=True) + EPS)
    return (y * g.astype(jnp.float32)).astype(x.dtype)


def axial_rope_tables(seq_len):
    rows = seq_len // GRID_W
    row = jnp.repeat(jnp.arange(rows), GRID_W)
    col = jnp.tile(jnp.arange(GRID_W), rows)
    inv = ROPE_THETA ** (-jnp.arange(0, ROPE_AXIS_DIM, 2, dtype=jnp.float32) / ROPE_AXIS_DIM)
    ang = jnp.stack([row, col], axis=1).astype(jnp.float32)[:, :, None] * inv
    return jnp.cos(ang)[:, None, :, None, :], jnp.sin(ang)[:, None, :, None, :]


def apply_rope(x, cos, sin):
    b, s, h, _ = x.shape
    xr = x.reshape(b, s, h, 2, 2, ROPE_FREQS).astype(jnp.float32)
    rot = jnp.stack([-xr[..., 1, :], xr[..., 0, :]], axis=-2)
    return (xr * cos + rot * sin).reshape(x.shape).astype(x.dtype)


def gqa_block_attention(q, k, v):
    b, s, kvh, g, hd = q.shape
    nb = s // Q_BLOCK
    qb = q.reshape(b, nb, Q_BLOCK, kvh, g, hd).transpose(1, 0, 2, 3, 4, 5)
    scale = HEAD_DIM ** -0.5

    def one_block(q_blk):
        sc = jnp.einsum('bqkgd,bskd->bkgqs', q_blk, k).astype(jnp.float32) * scale
        p = jax.nn.softmax(sc, axis=-1).astype(v.dtype)
        return jnp.einsum('bkgqs,bskd->bqkgd', p, v)

    o = lax.map(one_block, qb)
    return o.transpose(1, 0, 2, 3, 4, 5).reshape(b, s, kvh * g * hd)


def multiscale_pool(u):
    b, s, ng, c = u.shape
    uf = u.astype(jnp.float32)
    csum = jnp.concatenate([jnp.zeros((b, 1, ng, c), jnp.float32), jnp.cumsum(uf, axis=1)], axis=1)
    t = jnp.arange(s)
    outs = []
    for gi, w in enumerate(POOL_WINDOWS):
        lo = jnp.clip(t - w // 2, 0, s)
        hi = jnp.clip(t + (w - w // 2), 0, s)
        win_sum = csum[:, hi, gi] - csum[:, lo, gi]
        cnt = (hi - lo).astype(jnp.float32)[None, :, None]
        outs.append(win_sum / cnt - uf[:, :, gi])
    return jnp.stack(outs, axis=2).astype(u.dtype)


def setup_inputs(seed: int = 0) -> dict:
    key = jax.random.key(seed)
    ks = jax.random.split(key, 24)
    f32 = jnp.float32

    def w(k, shape, fan_in):
        return jax.random.normal(k, shape, f32) * (fan_in ** -0.5)

    def gain(k, shape):
        return 1.0 + 0.02 * jax.random.normal(k, shape, f32)

    L = DEPTH
    return {
        "x": jax.random.normal(ks[0], (BATCH, SEQ, D_MODEL), f32),
        "mem": jax.random.normal(ks[1], (BATCH, N_MEM, D_MODEL), f32),
        "g_mix": gain(ks[2], (L, D_MODEL)),
        "w_in": w(ks[3], (L, D_MODEL, IN_WIDTH), D_MODEL),
        "b_gate": 0.01 * jax.random.normal(ks[4], (L, N_BRANCHES * D_MODEL), f32),
        "g_q": gain(ks[5], (L, HEAD_DIM)),
        "g_k": gain(ks[6], (L, HEAD_DIM)),
        "w_attn_up": w(ks[7], (L, ATTN_WIDTH, D_MODEL), ATTN_WIDTH),
        "pool_w": w(ks[8], (L, N_POOL_GROUPS, POOL_GROUP, POOL_GROUP), POOL_GROUP),
        "pool_scale": gain(ks[9], (L, POOL_WIDTH)),
        "w_pool_up": w(ks[10], (L, POOL_WIDTH, D_MODEL), POOL_WIDTH),
        "w_out": w(ks[11], (L, D_MODEL, D_MODEL), D_MODEL),
        "g_cross": gain(ks[12], (L, D_MODEL)),
        "g_mem": gain(ks[13], (L, D_MODEL)),
        "w_xq": w(ks[14], (L, D_MODEL, D_MODEL), D_MODEL),
        "w_xkv": w(ks[15], (L, D_MODEL, 2 * D_MODEL), D_MODEL),
        "w_xo": w(ks[16], (L, D_MODEL, D_MODEL), D_MODEL),
        "g_ffn": gain(ks[17], (L, D_MODEL)),
        "w_ff1": w(ks[18], (L, D_MODEL, D_FF), D_MODEL),
        "w_ff2": w(ks[19], (L, D_FF, D_MODEL), D_FF),
        "g_final": gain(ks[20], (D_MODEL,)),
    }


def reference(x, mem, g_mix, w_in, b_gate, g_q, g_k, w_attn_up, pool_w, pool_scale, w_pool_up,
              w_out, g_cross, g_mem, w_xq, w_xkv, w_xo, g_ffn, w_ff1, w_ff2, g_final):
    b, s, d = x.shape
    m_len = mem.shape[1]
    cos, sin = axial_rope_tables(s)
    h = x
    for l in range(DEPTH):
        n1 = rmsnorm(h, g_mix[l])
        proj = n1 @ w_in[l]
        q, k, v, u, gates = jnp.split(
            proj,
            np.cumsum([ATTN_WIDTH, KV_WIDTH, KV_WIDTH, POOL_WIDTH]).tolist(),
            axis=-1)
        q = rmsnorm(q.reshape(b, s, N_Q_HEADS, HEAD_DIM), g_q[l])
        k = rmsnorm(k.reshape(b, s, N_KV_HEADS, HEAD_DIM), g_k[l])
        q = apply_rope(q, cos, sin).reshape(b, s, N_KV_HEADS, Q_PER_KV, HEAD_DIM)
        k = apply_rope(k, cos, sin)
        v = v.reshape(b, s, N_KV_HEADS, HEAD_DIM)
        a = gqa_block_attention(q, k, v) @ w_attn_up[l]
        pooled = multiscale_pool(u.reshape(b, s, N_POOL_GROUPS, POOL_GROUP))
        pm = jnp.einsum('bsgc,gcd->bsgd', pooled, pool_w[l]).reshape(b, s, POOL_WIDTH)
        p = (pm * pool_scale[l]) @ w_pool_up[l]
        gt = jax.nn.sigmoid(gates + b_gate[l])
        g_a, g_p = jnp.split(gt, 2, axis=-1)
        h = h + (g_a * a + g_p * p) @ w_out[l]

        n2 = rmsnorm(h, g_cross[l])
        mn = rmsnorm(mem, g_mem[l])
        xq = (n2 @ w_xq[l]).reshape(b, s, N_X_HEADS, X_HEAD_DIM)
        xk, xv = jnp.split((mn @ w_xkv[l]).reshape(b, m_len, 2, N_X_HEADS, X_HEAD_DIM), 2, axis=2)
        xk, xv = xk[:, :, 0], xv[:, :, 0]
        sc = jnp.einsum('bshd,bmhd->bhsm', xq, xk).astype(jnp.float32) * (X_HEAD_DIM ** -0.5)
        pr = jax.nn.softmax(sc, axis=-1).astype(xv.dtype)
        xo = jnp.einsum('bhsm,bmhd->bshd', pr, xv).reshape(b, s, d)
        h = h + xo @ w_xo[l]

        n3 = rmsnorm(h, g_ffn[l])
        h = h + jnp.square(jax.nn.relu(n3 @ w_ff1[l])) @ w_ff2[l]
    return rmsnorm(h, g_final)
```

```python
import functools

import jax
import jax.numpy as jnp
import numpy as np
from jax import lax
from jax.experimental import pallas as pl
from jax.experimental.pallas import tpu as pltpu

D_MODEL = 1024
GRID_W = 64
HEAD_DIM = 64
N_Q_HEADS = 8
N_KV_HEADS = 2
ATTN_WIDTH = N_Q_HEADS * HEAD_DIM
KV_WIDTH = N_KV_HEADS * HEAD_DIM
POOL_WINDOWS = (2, 4, 8, 16)
POOL_WIDTH = 512
POOL_GROUP = 128
ROPE_THETA = 10000.0
ROPE_FREQS = 16
N_X_HEADS = 4
X_HEAD_DIM = 256
D_FF = 4096
EPS = 1e-6

LANES = 128
POOL_HALO = 8
VMEM_LIMIT = 56 * 1024 * 1024

BF16 = jnp.bfloat16
F32 = jnp.float32

_Q0, _K0, _V0, _U0, _G0 = 0, 512, 640, 768, 1280


def _rms_scale(x):
    return lax.rsqrt(jnp.mean(x * x, axis=-1, keepdims=True) + EPS)


def _head_rms_scale(x128):
    lane = lax.broadcasted_iota(jnp.int32, x128.shape, 1)
    first = lane < HEAD_DIM
    sq = x128 * x128
    s_a = jnp.sum(jnp.where(first, sq, 0.0), axis=-1, keepdims=True)
    s_b = jnp.sum(jnp.where(first, 0.0, sq), axis=-1, keepdims=True)
    r_a = lax.rsqrt(s_a * (1.0 / HEAD_DIM) + EPS)
    r_b = lax.rsqrt(s_b * (1.0 / HEAD_DIM) + EPS)
    return jnp.where(first, r_a, r_b)


def _rope(xn, cos, sin_hi, sin_lo):
    return (xn * cos
            + pltpu.roll(xn, ROPE_FREQS, 1) * sin_hi
            + pltpu.roll(xn, LANES - ROPE_FREQS, 1) * sin_lo)


def _softmax_rows(s):
    m = jnp.max(s, axis=-1, keepdims=True)
    p = jnp.exp(s - m)
    return p, jnp.sum(p, axis=-1, keepdims=True)


def _mem_kv_kernel(mem_ref, g_ref, w_ref, xk_ref, xv_ref):
    m = mem_ref[...]
    mn = (m * _rms_scale(m) * g_ref[...]).astype(BF16)
    kv = jnp.dot(mn, w_ref[...], preferred_element_type=F32)
    xk_ref[...] = kv[:, :D_MODEL].astype(BF16)
    xv_ref[...] = kv[:, D_MODEL:].astype(BF16)


def _mem_kv(mem2d, g_mem, w_xkv):
    rows = mem2d.shape[0]
    tm = 1024
    const = lambda i: (0, 0)
    return pl.pallas_call(
        _mem_kv_kernel,
        grid=(rows // tm,),
        in_specs=[
            pl.BlockSpec((tm, D_MODEL), lambda i: (i, 0)),
            pl.BlockSpec((1, D_MODEL), const),
            pl.BlockSpec((D_MODEL, 2 * D_MODEL), const),
        ],
        out_specs=[
            pl.BlockSpec((tm, D_MODEL), lambda i: (i, 0)),
            pl.BlockSpec((tm, D_MODEL), lambda i: (i, 0)),
        ],
        out_shape=[jax.ShapeDtypeStruct((rows, D_MODEL), BF16)] * 2,
        compiler_params=pltpu.CompilerParams(
            dimension_semantics=("arbitrary",), vmem_limit_bytes=VMEM_LIMIT),
        name="mem_kv",
    )(mem2d, g_mem, w_xkv)


def _in_proj_kernel(x_ref, gmix_ref, w_ref, bg_ref, gq_ref, gk_ref, cos_ref, shi_ref, slo_ref,
                    q_ref, k_ref, v_ref, u_ref, gate_ref):
    x = x_ref[...]
    n1 = (x * _rms_scale(x) * gmix_ref[...]).astype(BF16)
    cos, shi, slo = cos_ref[...], shi_ref[...], slo_ref[...]

    def proj(c0, width):
        return jnp.dot(n1, w_ref[:, c0:c0 + width], preferred_element_type=F32)

    q = proj(_Q0, ATTN_WIDTH)
    gq = gq_ref[...]
    for t in range(ATTN_WIDTH // LANES):
        sl = slice(t * LANES, (t + 1) * LANES)
        qt = q[:, sl]
        qt = qt * _head_rms_scale(qt) * gq
        q_ref[:, sl] = _rope(qt, cos, shi, slo).astype(BF16)

    kv = proj(_K0, 2 * KV_WIDTH)
    kt = kv[:, :KV_WIDTH]
    kt = kt * _head_rms_scale(kt) * gk_ref[...]
    kt = _rope(kt, cos, shi, slo)
    lane = lax.broadcasted_iota(jnp.int32, kt.shape, 1)
    first = lane < HEAD_DIM
    swapped = pltpu.roll(kt, HEAD_DIM, 1)
    zero = jnp.zeros_like(kt)
    k_ref[:, 0 * LANES:1 * LANES] = jnp.where(first, kt, zero).astype(BF16)
    k_ref[:, 1 * LANES:2 * LANES] = jnp.where(first, zero, swapped).astype(BF16)
    k_ref[:, 2 * LANES:3 * LANES] = jnp.where(first, swapped, zero).astype(BF16)
    k_ref[:, 3 * LANES:4 * LANES] = jnp.where(first, zero, kt).astype(BF16)
    v_ref[...] = kv[:, KV_WIDTH:].astype(BF16)

    u_ref[...] = proj(_U0, POOL_WIDTH)

    chunk = 512
    for c in range(2 * D_MODEL // chunk):
        g = proj(_G0 + c * chunk, chunk) + bg_ref[:, c * chunk:(c + 1) * chunk]
        gate_ref[:, c * chunk:(c + 1) * chunk] = jax.nn.sigmoid(g).astype(BF16)


def _in_proj(x2d, g_mix, w_in, b_gate, gq_t, gk_t, cos_t, shi_t, slo_t, seq):
    rows = x2d.shape[0]
    tm = 512
    seq_tiles = seq // tm
    const = lambda i: (0, 0)
    row = lambda i: (i, 0)
    pos = lambda i: (i % seq_tiles, 0)
    in_width = w_in.shape[1]
    return pl.pallas_call(
        _in_proj_kernel,
        grid=(rows // tm,),
        in_specs=[
            pl.BlockSpec((tm, D_MODEL), row),
            pl.BlockSpec((1, D_MODEL), const),
            pl.BlockSpec((D_MODEL, in_width), const),
            pl.BlockSpec((1, 2 * D_MODEL), const),
            pl.BlockSpec((1, LANES), const),
            pl.BlockSpec((1, LANES), const),
            pl.BlockSpec((tm, LANES), pos),
            pl.BlockSpec((tm, LANES), pos),
            pl.BlockSpec((tm, LANES), pos),
        ],
        out_specs=[
            pl.BlockSpec((tm, ATTN_WIDTH), row),
            pl.BlockSpec((tm, 4 * LANES), row),
            pl.BlockSpec((tm, KV_WIDTH), row),
            pl.BlockSpec((tm, POOL_WIDTH), row),
            pl.BlockSpec((tm, 2 * D_MODEL), row),
        ],
        out_shape=[
            jax.ShapeDtypeStruct((rows, ATTN_WIDTH), BF16),
            jax.ShapeDtypeStruct((rows, 4 * LANES), BF16),
            jax.ShapeDtypeStruct((rows, KV_WIDTH), BF16),
            jax.ShapeDtypeStruct((rows, POOL_WIDTH), F32),
            jax.ShapeDtypeStruct((rows, 2 * D_MODEL), BF16),
        ],
        compiler_params=pltpu.CompilerParams(
            dimension_semantics=("arbitrary",), vmem_limit_bytes=VMEM_LIMIT),
        name="in_proj",
    )(x2d, g_mix, w_in, b_gate, gq_t, gk_t, cos_t, shi_t, slo_t)


def _mixer_kernel(q_ref, k_ref, v_ref, u_ref, uprev_ref, unext_ref, gate_ref, x_ref,
                  wau_ref, pw_ref, ps_ref, wpu_ref, wo_ref, h_ref, slab_ref, *, tq, seq):
    i = pl.program_id(1)
    n_tiles = seq // tq

    v = v_ref[...]
    lane = lax.broadcasted_iota(jnp.int32, (tq, LANES), 1)
    first = lane < HEAD_DIM
    heads = []
    for h in range(N_Q_HEADS):
        g = h // (N_Q_HEADS // N_KV_HEADS)
        q_pair = q_ref[:, (h // 2) * LANES:(h // 2 + 1) * LANES]
        k_sel = k_ref[:, (2 * g + h % 2) * LANES:(2 * g + h % 2 + 1) * LANES]
        s = lax.dot_general(q_pair, k_sel, (((1,), (1,)), ((), ())), preferred_element_type=F32)
        p, l = _softmax_rows(s)
        o = jnp.dot(p.astype(BF16), v, preferred_element_type=F32)
        heads.append(o / l)
    half = N_Q_HEADS // 2
    o_perm = jnp.concatenate(
        [jnp.where(first, heads[t], heads[half + t]) for t in range(half)], axis=1)
    a = jnp.dot(o_perm.astype(BF16), wau_ref[...], preferred_element_type=F32)

    zero_halo = jnp.zeros((POOL_HALO, POOL_WIDTH), F32)
    slab_ref[0:POOL_HALO, :] = jnp.where(i > 0, uprev_ref[...], zero_halo)
    slab_ref[POOL_HALO:POOL_HALO + tq, :] = u_ref[...]
    slab_ref[POOL_HALO + tq:, :] = jnp.where(i < n_tiles - 1, unext_ref[...], zero_halo)
    t_pos = i * tq + lax.broadcasted_iota(jnp.int32, (tq, 1), 0)
    pm = []
    for gi, w in enumerate(POOL_WINDOWS):
        cols = slice(gi * POOL_GROUP, (gi + 1) * POOL_GROUP)
        win = None
        for j in range(-(w // 2), w - w // 2):
            piece = slab_ref[POOL_HALO + j:POOL_HALO + j + tq, cols]
            win = piece if win is None else win + piece
        lo = jnp.maximum(t_pos - w // 2, 0)
        hi = jnp.minimum(t_pos + (w - w // 2), seq)
        cnt = (hi - lo).astype(F32)
        pooled = win / cnt - u_ref[:, cols]
        pm.append(jnp.dot(pooled.astype(BF16), pw_ref[gi], preferred_element_type=F32))
    pm = jnp.concatenate(pm, axis=1) * ps_ref[...]
    p_branch = jnp.dot(pm.astype(BF16), wpu_ref[...], preferred_element_type=F32)

    g_a = gate_ref[:, :D_MODEL].astype(F32)
    g_p = gate_ref[:, D_MODEL:].astype(F32)
    mixed = (g_a * a + g_p * p_branch).astype(BF16)
    h_ref[...] = x_ref[...] + jnp.dot(mixed, wo_ref[...], preferred_element_type=F32)


def _mixer(q, k, v, u, gates, x2d, wau, pw, ps, wpu, wo, batch, seq):
    tq = 256
    n_tiles = seq // tq
    halo_blocks = tq // POOL_HALO
    last_halo = seq // POOL_HALO - 1
    row = lambda b, i: (b * n_tiles + i, 0)
    per_batch = lambda b, i: (b, 0)
    const2 = lambda b, i: (0, 0)
    const3 = lambda b, i: (0, 0, 0)
    prev = lambda b, i: (b * (seq // POOL_HALO) + jnp.maximum(i * halo_blocks - 1, 0), 0)
    nxt = lambda b, i: (b * (seq // POOL_HALO) + jnp.minimum((i + 1) * halo_blocks, last_halo), 0)
    return pl.pallas_call(
        functools.partial(_mixer_kernel, tq=tq, seq=seq),
        grid=(batch, n_tiles),
        in_specs=[
            pl.BlockSpec((tq, ATTN_WIDTH), row),
            pl.BlockSpec((seq, 4 * LANES), per_batch),
            pl.BlockSpec((seq, KV_WIDTH), per_batch),
            pl.BlockSpec((tq, POOL_WIDTH), row),
            pl.BlockSpec((POOL_HALO, POOL_WIDTH), prev),
            pl.BlockSpec((POOL_HALO, POOL_WIDTH), nxt),
            pl.BlockSpec((tq, 2 * D_MODEL), row),
            pl.BlockSpec((tq, D_MODEL), row),
            pl.BlockSpec((ATTN_WIDTH, D_MODEL), const2),
            pl.BlockSpec((len(POOL_WINDOWS), POOL_GROUP, POOL_GROUP), const3),
            pl.BlockSpec((1, POOL_WIDTH), const2),
            pl.BlockSpec((POOL_WIDTH, D_MODEL), const2),
            pl.BlockSpec((D_MODEL, D_MODEL), const2),
        ],
        out_specs=pl.BlockSpec((tq, D_MODEL), row),
        out_shape=jax.ShapeDtypeStruct(x2d.shape, F32),
        scratch_shapes=[pltpu.VMEM((tq + 2 * POOL_HALO, POOL_WIDTH), F32)],
        compiler_params=pltpu.CompilerParams(
            dimension_semantics=("arbitrary", "arbitrary"), vmem_limit_bytes=VMEM_LIMIT),
        name="mixer",
    )(q, k, v, u, u, u, gates, x2d, wau, pw, ps, wpu, wo)


def _cross_ffn_kernel(h_ref, xk_ref, xv_ref, gc_ref, wq_ref, wo_ref, gf_ref, w1_ref, w2_ref,
                      gfin_ref, out_ref):
    h = h_ref[...]
    n2 = (h * _rms_scale(h) * gc_ref[...]).astype(BF16)
    xq = jnp.dot(n2, wq_ref[...], preferred_element_type=F32) * (X_HEAD_DIM ** -0.5)
    xq = xq.astype(BF16)
    outs = []
    for hd in range(N_X_HEADS):
        cols = slice(hd * X_HEAD_DIM, (hd + 1) * X_HEAD_DIM)
        s = lax.dot_general(xq[:, cols], xk_ref[:, cols], (((1,), (1,)), ((), ())),
                            preferred_element_type=F32)
        p, l = _softmax_rows(s)
        o = jnp.dot(p.astype(BF16), xv_ref[:, cols], preferred_element_type=F32)
        outs.append((o / l).astype(BF16))
    xo = jnp.concatenate(outs, axis=1)
    h = h + jnp.dot(xo, wo_ref[...], preferred_element_type=F32)

    n3 = (h * _rms_scale(h) * gf_ref[...]).astype(BF16)
    chunk = 1024
    acc = h
    for c in range(D_FF // chunk):
        t = jnp.dot(n3, w1_ref[:, c * chunk:(c + 1) * chunk], preferred_element_type=F32)
        t = jnp.square(jnp.maximum(t, 0.0)).astype(BF16)
        acc = acc + jnp.dot(t, w2_ref[c * chunk:(c + 1) * chunk, :], preferred_element_type=F32)
    out_ref[...] = acc * _rms_scale(acc) * gfin_ref[...]


def _cross_ffn(h2d, xk, xv, g_cross, w_xq, w_xo, g_ffn, w_ff1, w_ff2, g_final, seq, n_mem):
    rows = h2d.shape[0]
    tm = 512
    tiles_per_batch = seq // tm
    const = lambda i: (0, 0)
    row = lambda i: (i, 0)
    per_batch = lambda i: (i // tiles_per_batch, 0)
    single = pl.Buffered(1)
    return pl.pallas_call(
        _cross_ffn_kernel,
        grid=(rows // tm,),
        in_specs=[
            pl.BlockSpec((tm, D_MODEL), row),
            pl.BlockSpec((n_mem, D_MODEL), per_batch),
            pl.BlockSpec((n_mem, D_MODEL), per_batch),
            pl.BlockSpec((1, D_MODEL), const),
            pl.BlockSpec((D_MODEL, D_MODEL), const, pipeline_mode=single),
            pl.BlockSpec((D_MODEL, D_MODEL), const, pipeline_mode=single),
            pl.BlockSpec((1, D_MODEL), const),
            pl.BlockSpec((D_MODEL, D_FF), const, pipeline_mode=single),
            pl.BlockSpec((D_FF, D_MODEL), const, pipeline_mode=single),
            pl.BlockSpec((1, D_MODEL), const),
        ],
        out_specs=pl.BlockSpec((tm, D_MODEL), row),
        out_shape=jax.ShapeDtypeStruct(h2d.shape, F32),
        compiler_params=pltpu.CompilerParams(
            dimension_semantics=("arbitrary",), vmem_limit_bytes=VMEM_LIMIT),
        name="cross_ffn",
    )(h2d, xk, xv, g_cross, w_xq, w_xo, g_ffn, w_ff1, w_ff2, g_final)


def _rope_tables(seq):
    t = jnp.arange(seq)
    inv = ROPE_THETA ** (-jnp.arange(0, 2 * ROPE_FREQS, 2, dtype=F32) / (2 * ROPE_FREQS))
    ang_row = (t // GRID_W).astype(F32)[:, None] * inv
    ang_col = (t % GRID_W).astype(F32)[:, None] * inv
    zeros = jnp.zeros_like(ang_row)

    def head(row_part, col_part):
        return jnp.concatenate([row_part[0], row_part[1], col_part[0], col_part[1]], axis=1)

    cos = head((jnp.cos(ang_row),) * 2, (jnp.cos(ang_col),) * 2)
    sin_hi = head((zeros, jnp.sin(ang_row)), (zeros, jnp.sin(ang_col)))
    sin_lo = head((-jnp.sin(ang_row), zeros), (-jnp.sin(ang_col), zeros))
    two = lambda a: jnp.concatenate([a, a], axis=1)
    return two(cos), two(sin_hi), two(sin_lo)


def kernel(x, mem, g_mix, w_in, b_gate, g_q, g_k, w_attn_up, pool_w, pool_scale, w_pool_up, w_out,
           g_cross, g_mem, w_xq, w_xkv, w_xo, g_ffn, w_ff1, w_ff2, g_final):
    batch, seq, d = x.shape
    n_mem = mem.shape[1]
    assert d == D_MODEL and w_in.shape[0] == 1 and seq % 512 == 0 and seq % GRID_W == 0
    l = 0
    x2d = x.reshape(batch * seq, d)
    mem2d = mem.reshape(batch * n_mem, d)
    row = lambda a: a.reshape(1, -1)

    cos_t, shi_t, slo_t = _rope_tables(seq)
    two_heads = lambda g: jnp.concatenate([g, g]).reshape(1, LANES)
    gq_t = two_heads(g_q[l] * (HEAD_DIM ** -0.5))
    gk_t = two_heads(g_k[l])
    half = N_Q_HEADS // 2
    wau = (w_attn_up[l].reshape(2, half, HEAD_DIM, d).transpose(1, 0, 2, 3)
           .reshape(ATTN_WIDTH, d).astype(BF16))

    xk, xv = _mem_kv(mem2d, row(g_mem[l]), w_xkv[l].astype(BF16))
    q, k, v, u, gates = _in_proj(x2d, row(g_mix[l]), w_in[l].astype(BF16), row(b_gate[l]),
                                 gq_t, gk_t, cos_t, shi_t, slo_t, seq)
    h1 = _mixer(q, k, v, u, gates, x2d, wau, pool_w[l].astype(BF16), row(pool_scale[l]),
                w_pool_up[l].astype(BF16), w_out[l].astype(BF16), batch, seq)
    out = _cross_ffn(h1, xk, xv, row(g_cross[l]), w_xq[l].astype(BF16), w_xo[l].astype(BF16),
                     row(g_ffn[l]), w_ff1[l].astype(BF16), w_ff2[l].astype(BF16), row(g_final),
                     seq, n_mem)
    return out.reshape(batch, seq, d)
```

```python
import functools
import math

import jax
import jax.numpy as jnp
from jax import lax
from jax.experimental import pallas as pl
from jax.experimental.pallas import tpu as pltpu

D_MODEL = 1024
GRID_W = 64
HEAD_DIM = 64
N_Q_HEADS = 8
N_KV_HEADS = 2
ATTN_WIDTH = N_Q_HEADS * HEAD_DIM
KV_WIDTH = N_KV_HEADS * HEAD_DIM
POOL_WINDOWS = (2, 4, 8, 16)
POOL_WIDTH = 512
POOL_GROUP = 128
ROPE_THETA = 10000.0
ROPE_FREQS = 16
N_X_HEADS = 4
X_HEAD_DIM = 256
D_FF = 4096
EPS = 1e-6

LANES = 128
POOL_HALO = 8
VMEM_LIMIT = 56 * 1024 * 1024
LOG2_E = math.log2(math.e)
MAX_FIXED_SHIFT = 40.0

BF16 = jnp.bfloat16
F32 = jnp.float32

_Q0, _K0, _V0, _U0, _G0 = 0, 512, 640, 768, 1280


def _rms_scale(x):
    return lax.rsqrt(jnp.mean(x * x, axis=-1, keepdims=True) + EPS)


def _head_rms_scale(x128):
    lane = lax.broadcasted_iota(jnp.int32, x128.shape, 1)
    first = lane < HEAD_DIM
    sq = x128 * x128
    s_a = jnp.sum(jnp.where(first, sq, 0.0), axis=-1, keepdims=True)
    s_b = jnp.sum(jnp.where(first, 0.0, sq), axis=-1, keepdims=True)
    r_a = lax.rsqrt(s_a * (1.0 / HEAD_DIM) + EPS)
    r_b = lax.rsqrt(s_b * (1.0 / HEAD_DIM) + EPS)
    return jnp.where(first, r_a, r_b)


def _rope(xn, cos, sin_hi, sin_lo):
    return (xn * cos
            + pltpu.roll(xn, ROPE_FREQS, 1) * sin_hi
            + pltpu.roll(xn, LANES - ROPE_FREQS, 1) * sin_lo)


def _softmax_rows(s):
    m = jnp.max(s, axis=-1, keepdims=True)
    p = jnp.exp(s - m)
    return p, jnp.sum(p, axis=-1, keepdims=True)


def _mem_kv_kernel(mem_ref, g_ref, w_ref, xk_ref, xv_ref):
    m = mem_ref[...]
    mn = (m * _rms_scale(m) * g_ref[...]).astype(BF16)
    kv = jnp.dot(mn, w_ref[...], preferred_element_type=F32)
    xk_ref[...] = kv[:, :D_MODEL].astype(BF16)
    xv_ref[...] = kv[:, D_MODEL:].astype(BF16)


def _mem_kv(mem2d, g_mem, w_xkv):
    rows = mem2d.shape[0]
    tm = 1024
    const = lambda i: (0, 0)
    return pl.pallas_call(
        _mem_kv_kernel,
        grid=(rows // tm,),
        in_specs=[
            pl.BlockSpec((tm, D_MODEL), lambda i: (i, 0)),
            pl.BlockSpec((1, D_MODEL), const),
            pl.BlockSpec((D_MODEL, 2 * D_MODEL), const),
        ],
        out_specs=[
            pl.BlockSpec((tm, D_MODEL), lambda i: (i, 0)),
            pl.BlockSpec((tm, D_MODEL), lambda i: (i, 0)),
        ],
        out_shape=[jax.ShapeDtypeStruct((rows, D_MODEL), BF16)] * 2,
        compiler_params=pltpu.CompilerParams(
            dimension_semantics=("arbitrary",), vmem_limit_bytes=VMEM_LIMIT),
        name="mem_kv",
    )(mem2d, g_mem, w_xkv)


def _in_proj_kernel(x_ref, gmix_ref, w_ref, bg_ref, gq_ref, gk_ref, qshift_ref,
                    cos_ref, shi_ref, slo_ref, q_ref, k_ref, v_ref, u_ref, gate_ref):
    x = x_ref[...]
    n1 = (x * _rms_scale(x) * gmix_ref[...]).astype(BF16)
    cos, shi, slo = cos_ref[...], shi_ref[...], slo_ref[...]
    lane = lax.broadcasted_iota(jnp.int32, (x.shape[0], LANES), 1)
    first = lane < HEAD_DIM

    def proj(c0, width):
        return jnp.dot(n1, w_ref[:, c0:c0 + width], preferred_element_type=F32)

    q = proj(_Q0, ATTN_WIDTH)
    gq = gq_ref[...]
    qshift = qshift_ref[...]
    for t in range(ATTN_WIDTH // LANES):
        qt = q[:, t * LANES:(t + 1) * LANES]
        qt = _rope(qt * _head_rms_scale(qt) * gq, cos, shi, slo)
        q_ref[:, (2 * t) * LANES:(2 * t + 1) * LANES] = jnp.where(first, qt, qshift).astype(BF16)
        q_ref[:, (2 * t + 1) * LANES:(2 * t + 2) * LANES] = jnp.where(
            first, pltpu.roll(qt, HEAD_DIM, 1), qshift).astype(BF16)

    kv = proj(_K0, 2 * KV_WIDTH)
    kt = kv[:, :KV_WIDTH]
    kt = _rope(kt * _head_rms_scale(kt) * gk_ref[...], cos, shi, slo)
    one_at_head_dim = jnp.where(lane == HEAD_DIM, 1.0, 0.0)
    k_ref[:, :LANES] = jnp.where(first, kt, one_at_head_dim).astype(BF16)
    k_ref[:, LANES:] = jnp.where(first, pltpu.roll(kt, HEAD_DIM, 1), one_at_head_dim).astype(BF16)
    v_ref[:, :LANES] = kv[:, KV_WIDTH:].astype(BF16)
    v_ref[:, LANES:] = jnp.where(lane == 0, 1.0, 0.0).astype(BF16)

    u_ref[...] = proj(_U0, POOL_WIDTH)

    chunk = 512
    for c in range(2 * D_MODEL // chunk):
        g = proj(_G0 + c * chunk, chunk) + bg_ref[:, c * chunk:(c + 1) * chunk]
        gate_ref[:, c * chunk:(c + 1) * chunk] = jax.nn.sigmoid(g).astype(BF16)


def _in_proj(x2d, g_mix, w_in, b_gate, gq_t, gk_t, qshift, cos_t, shi_t, slo_t, seq):
    rows = x2d.shape[0]
    tm = 512
    seq_tiles = seq // tm
    const = lambda i: (0, 0)
    row = lambda i: (i, 0)
    pos = lambda i: (i % seq_tiles, 0)
    in_width = w_in.shape[1]
    return pl.pallas_call(
        _in_proj_kernel,
        grid=(rows // tm,),
        in_specs=[
            pl.BlockSpec((tm, D_MODEL), row),
            pl.BlockSpec((1, D_MODEL), const),
            pl.BlockSpec((D_MODEL, in_width), const),
            pl.BlockSpec((1, 2 * D_MODEL), const),
            pl.BlockSpec((1, LANES), const),
            pl.BlockSpec((1, LANES), const),
            pl.BlockSpec((1, LANES), const),
            pl.BlockSpec((tm, LANES), pos),
            pl.BlockSpec((tm, LANES), pos),
            pl.BlockSpec((tm, LANES), pos),
        ],
        out_specs=[
            pl.BlockSpec((tm, N_Q_HEADS * LANES), row),
            pl.BlockSpec((tm, N_KV_HEADS * LANES), row),
            pl.BlockSpec((tm, 2 * LANES), row),
            pl.BlockSpec((tm, POOL_WIDTH), row),
            pl.BlockSpec((tm, 2 * D_MODEL), row),
        ],
        out_shape=[
            jax.ShapeDtypeStruct((rows, N_Q_HEADS * LANES), BF16),
            jax.ShapeDtypeStruct((rows, N_KV_HEADS * LANES), BF16),
            jax.ShapeDtypeStruct((rows, 2 * LANES), BF16),
            jax.ShapeDtypeStruct((rows, POOL_WIDTH), F32),
            jax.ShapeDtypeStruct((rows, 2 * D_MODEL), BF16),
        ],
        compiler_params=pltpu.CompilerParams(
            dimension_semantics=("arbitrary",), vmem_limit_bytes=VMEM_LIMIT),
        name="in_proj",
    )(x2d, g_mix, w_in, b_gate, gq_t, gk_t, qshift, cos_t, shi_t, slo_t)


def _mixer_kernel(q_ref, k_ref, v_ref, u_ref, uprev_ref, unext_ref, gate_ref, x_ref,
                  wau_ref, pw_ref, ps_ref, wpu_ref, wo_ref, h_ref, slab_ref,
                  *, tq, seq, subtract_max):
    i = pl.program_id(1)
    n_tiles = seq // tq

    v = v_ref[...]
    lane = lax.broadcasted_iota(jnp.int32, (tq, LANES), 1)
    first = lane < HEAD_DIM
    heads = []
    for h in range(N_Q_HEADS):
        g = h // (N_Q_HEADS // N_KV_HEADS)
        s = lax.dot_general(q_ref[:, h * LANES:(h + 1) * LANES], k_ref[:, g * LANES:(g + 1) * LANES],
                            (((1,), (1,)), ((), ())), preferred_element_type=F32)
        if subtract_max:
            s = s - jnp.max(s, axis=-1, keepdims=True)
        p = jnp.exp2(s).astype(BF16)
        r = jnp.dot(p, v, preferred_element_type=F32)
        heads.append(r[:, :LANES] / r[:, LANES:LANES + 1])
    half = N_Q_HEADS // 2
    o_perm = jnp.concatenate(
        [jnp.where(first, heads[t], heads[half + t]) for t in range(half)], axis=1)
    a = jnp.dot(o_perm.astype(BF16), wau_ref[...], preferred_element_type=F32)

    zero_halo = jnp.zeros((POOL_HALO, POOL_WIDTH), F32)
    slab_ref[0:POOL_HALO, :] = jnp.where(i > 0, uprev_ref[...], zero_halo)
    slab_ref[POOL_HALO:POOL_HALO + tq, :] = u_ref[...]
    slab_ref[POOL_HALO + tq:, :] = jnp.where(i < n_tiles - 1, unext_ref[...], zero_halo)
    t_pos = i * tq + lax.broadcasted_iota(jnp.int32, (tq, 1), 0)
    pm = []
    for gi, w in enumerate(POOL_WINDOWS):
        cols = slice(gi * POOL_GROUP, (gi + 1) * POOL_GROUP)
        win = None
        for j in range(-(w // 2), w - w // 2):
            piece = slab_ref[POOL_HALO + j:POOL_HALO + j + tq, cols]
            win = piece if win is None else win + piece
        lo = jnp.maximum(t_pos - w // 2, 0)
        hi = jnp.minimum(t_pos + (w - w // 2), seq)
        cnt = (hi - lo).astype(F32)
        pooled = win / cnt - u_ref[:, cols]
        pm.append(jnp.dot(pooled.astype(BF16), pw_ref[gi], preferred_element_type=F32))
    pm = jnp.concatenate(pm, axis=1) * ps_ref[...]
    p_branch = jnp.dot(pm.astype(BF16), wpu_ref[...], preferred_element_type=F32)

    g_a = gate_ref[:, :D_MODEL].astype(F32)
    g_p = gate_ref[:, D_MODEL:].astype(F32)
    mixed = (g_a * a + g_p * p_branch).astype(BF16)
    h_ref[...] = x_ref[...] + jnp.dot(mixed, wo_ref[...], preferred_element_type=F32)


def _mixer(q, k, v, u, gates, x2d, wau, pw, ps, wpu, wo, *, batch, seq, subtract_max):
    tq = 256
    n_tiles = seq // tq
    halo_blocks = tq // POOL_HALO
    last_halo = seq // POOL_HALO - 1
    row = lambda b, i: (b * n_tiles + i, 0)
    per_batch = lambda b, i: (b, 0)
    const2 = lambda b, i: (0, 0)
    const3 = lambda b, i: (0, 0, 0)
    prev = lambda b, i: (b * (seq // POOL_HALO) + jnp.maximum(i * halo_blocks - 1, 0), 0)
    nxt = lambda b, i: (b * (seq // POOL_HALO) + jnp.minimum((i + 1) * halo_blocks, last_halo), 0)
    return pl.pallas_call(
        functools.partial(_mixer_kernel, tq=tq, seq=seq, subtract_max=subtract_max),
        grid=(batch, n_tiles),
        in_specs=[
            pl.BlockSpec((tq, N_Q_HEADS * LANES), row),
            pl.BlockSpec((seq, N_KV_HEADS * LANES), per_batch),
            pl.BlockSpec((seq, 2 * LANES), per_batch),
            pl.BlockSpec((tq, POOL_WIDTH), row),
            pl.BlockSpec((POOL_HALO, POOL_WIDTH), prev),
            pl.BlockSpec((POOL_HALO, POOL_WIDTH), nxt),
            pl.BlockSpec((tq, 2 * D_MODEL), row),
            pl.BlockSpec((tq, D_MODEL), row),
            pl.BlockSpec((ATTN_WIDTH, D_MODEL), const2),
            pl.BlockSpec((len(POOL_WINDOWS), POOL_GROUP, POOL_GROUP), const3),
            pl.BlockSpec((1, POOL_WIDTH), const2),
            pl.BlockSpec((POOL_WIDTH, D_MODEL), const2),
            pl.BlockSpec((D_MODEL, D_MODEL), const2),
        ],
        out_specs=pl.BlockSpec((tq, D_MODEL), row),
        out_shape=jax.ShapeDtypeStruct(x2d.shape, F32),
        scratch_shapes=[pltpu.VMEM((tq + 2 * POOL_HALO, POOL_WIDTH), F32)],
        compiler_params=pltpu.CompilerParams(
            dimension_semantics=("arbitrary", "arbitrary"), vmem_limit_bytes=VMEM_LIMIT),
        name="mixer_max" if subtract_max else "mixer",
    )(q, k, v, u, u, u, gates, x2d, wau, pw, ps, wpu, wo)


def _cross_ffn_kernel(h_ref, xk_ref, xv_ref, gc_ref, wq_ref, wo_ref, gf_ref, w1_ref, w2_ref,
                      gfin_ref, out_ref):
    h = h_ref[...]
    n2 = (h * _rms_scale(h) * gc_ref[...]).astype(BF16)
    xq = jnp.dot(n2, wq_ref[...], preferred_element_type=F32) * (X_HEAD_DIM ** -0.5)
    xq = xq.astype(BF16)
    outs = []
    for hd in range(N_X_HEADS):
        cols = slice(hd * X_HEAD_DIM, (hd + 1) * X_HEAD_DIM)
        s = lax.dot_general(xq[:, cols], xk_ref[:, cols], (((1,), (1,)), ((), ())),
                            preferred_element_type=F32)
        p, l = _softmax_rows(s)
        o = jnp.dot(p.astype(BF16), xv_ref[:, cols], preferred_element_type=F32)
        outs.append((o / l).astype(BF16))
    xo = jnp.concatenate(outs, axis=1)
    h = h + jnp.dot(xo, wo_ref[...], preferred_element_type=F32)

    n3 = (h * _rms_scale(h) * gf_ref[...]).astype(BF16)
    chunk = 1024
    acc = h
    for c in range(D_FF // chunk):
        t = jnp.dot(n3, w1_ref[:, c * chunk:(c + 1) * chunk], preferred_element_type=F32)
        t = jnp.square(jnp.maximum(t, 0.0)).astype(BF16)
        acc = acc + jnp.dot(t, w2_ref[c * chunk:(c + 1) * chunk, :], preferred_element_type=F32)
    out_ref[...] = acc * _rms_scale(acc) * gfin_ref[...]


def _cross_ffn(h2d, xk, xv, g_cross, w_xq, w_xo, g_ffn, w_ff1, w_ff2, g_final, seq, n_mem):
    rows = h2d.shape[0]
    tm = 512
    tiles_per_batch = seq // tm
    const = lambda i: (0, 0)
    row = lambda i: (i, 0)
    per_batch = lambda i: (i // tiles_per_batch, 0)
    single = pl.Buffered(1)
    return pl.pallas_call(
        _cross_ffn_kernel,
        grid=(rows // tm,),
        in_specs=[
            pl.BlockSpec((tm, D_MODEL), row),
            pl.BlockSpec((n_mem, D_MODEL), per_batch),
            pl.BlockSpec((n_mem, D_MODEL), per_batch),
            pl.BlockSpec((1, D_MODEL), const),
            pl.BlockSpec((D_MODEL, D_MODEL), const, pipeline_mode=single),
            pl.BlockSpec((D_MODEL, D_MODEL), const, pipeline_mode=single),
            pl.BlockSpec((1, D_MODEL), const),
            pl.BlockSpec((D_MODEL, D_FF), const, pipeline_mode=single),
            pl.BlockSpec((D_FF, D_MODEL), const, pipeline_mode=single),
            pl.BlockSpec((1, D_MODEL), const),
        ],
        out_specs=pl.BlockSpec((tm, D_MODEL), row),
        out_shape=jax.ShapeDtypeStruct(h2d.shape, F32),
        compiler_params=pltpu.CompilerParams(
            dimension_semantics=("arbitrary",), vmem_limit_bytes=VMEM_LIMIT),
        name="cross_ffn",
    )(h2d, xk, xv, g_cross, w_xq, w_xo, g_ffn, w_ff1, w_ff2, g_final)


def _rope_tables(seq):
    t = jnp.arange(seq)
    inv = ROPE_THETA ** (-jnp.arange(0, 2 * ROPE_FREQS, 2, dtype=F32) / (2 * ROPE_FREQS))
    ang_row = (t // GRID_W).astype(F32)[:, None] * inv
    ang_col = (t % GRID_W).astype(F32)[:, None] * inv
    zeros = jnp.zeros_like(ang_row)

    def head(row_part, col_part):
        return jnp.concatenate([row_part[0], row_part[1], col_part[0], col_part[1]], axis=1)

    cos = head((jnp.cos(ang_row),) * 2, (jnp.cos(ang_col),) * 2)
    sin_hi = head((zeros, jnp.sin(ang_row)), (zeros, jnp.sin(ang_col)))
    sin_lo = head((-jnp.sin(ang_row), zeros), (-jnp.sin(ang_col), zeros))
    two = lambda a: jnp.concatenate([a, a], axis=1)
    return two(cos), two(sin_hi), two(sin_lo)


def kernel(x, mem, g_mix, w_in, b_gate, g_q, g_k, w_attn_up, pool_w, pool_scale, w_pool_up, w_out,
           g_cross, g_mem, w_xq, w_xkv, w_xo, g_ffn, w_ff1, w_ff2, g_final):
    batch, seq, d = x.shape
    n_mem = mem.shape[1]
    assert d == D_MODEL and w_in.shape[0] == 1 and seq % 512 == 0 and seq % GRID_W == 0
    l = 0
    x2d = x.reshape(batch * seq, d)
    mem2d = mem.reshape(batch * n_mem, d)
    row = lambda a: a.reshape(1, -1)

    cos_t, shi_t, slo_t = _rope_tables(seq)
    two_heads = lambda g: jnp.concatenate([g, g]).reshape(1, LANES)
    gq_t = two_heads(g_q[l] * (HEAD_DIM ** -0.5 * LOG2_E))
    gk_t = two_heads(g_k[l])
    shift = 1.02 * HEAD_DIM ** 0.5 * jnp.max(jnp.abs(g_q[l])) * jnp.max(jnp.abs(g_k[l])) * LOG2_E
    fixed_shift_ok = shift <= MAX_FIXED_SHIFT * LOG2_E
    qshift = jnp.where((jnp.arange(LANES) == HEAD_DIM) & fixed_shift_ok, -shift, 0.0)
    qshift = qshift.astype(F32).reshape(1, LANES)
    half = N_Q_HEADS // 2
    wau = (w_attn_up[l].reshape(2, half, HEAD_DIM, d).transpose(1, 0, 2, 3)
           .reshape(ATTN_WIDTH, d).astype(BF16))

    xk, xv = _mem_kv(mem2d, row(g_mem[l]), w_xkv[l].astype(BF16))
    q, k, v, u, gates = _in_proj(x2d, row(g_mix[l]), w_in[l].astype(BF16), row(b_gate[l]),
                                 gq_t, gk_t, qshift, cos_t, shi_t, slo_t, seq)
    mixer_args = (q, k, v, u, gates, x2d, wau, pool_w[l].astype(BF16), row(pool_scale[l]),
                  w_pool_up[l].astype(BF16), w_out[l].astype(BF16))
    h1 = lax.cond(
        fixed_shift_ok,
        lambda *a: _mixer(*a, batch=batch, seq=seq, subtract_max=False),
        lambda *a: _mixer(*a, batch=batch, seq=seq, subtract_max=True),
        *mixer_args)
    out = _cross_ffn(h1, xk, xv, row(g_cross[l]), w_xq[l].astype(BF16), w_xo[l].astype(BF16),
                     row(g_ffn[l]), w_ff1[l].astype(BF16), w_ff2[l].astype(BF16), row(g_final),
                     seq, n_mem)
    return out.reshape(batch, seq, d)
```

```python
import functools
import math

import jax
import jax.numpy as jnp
from jax import lax
from jax.experimental import pallas as pl
from jax.experimental.pallas import tpu as pltpu

D_MODEL = 1024
GRID_W = 64
HEAD_DIM = 64
N_Q_HEADS = 8
N_KV_HEADS = 2
ATTN_WIDTH = N_Q_HEADS * HEAD_DIM
KV_WIDTH = N_KV_HEADS * HEAD_DIM
POOL_WINDOWS = (2, 4, 8, 16)
POOL_WIDTH = 512
POOL_GROUP = 128
ROPE_THETA = 10000.0
ROPE_FREQS = 16
N_X_HEADS = 4
X_HEAD_DIM = 256
D_FF = 4096
EPS = 1e-6

LANES = 128
POOL_HALO = 8
VMEM_LIMIT = 56 * 1024 * 1024
LOG2_E = math.log2(math.e)
MAX_FIXED_SHIFT = 40.0

BF16 = jnp.bfloat16
F32 = jnp.float32

_Q0, _K0, _V0, _U0, _G0 = 0, 512, 640, 768, 1280


def _rms_scale(x):
    return lax.rsqrt(jnp.mean(x * x, axis=-1, keepdims=True) + EPS)


def _head_rms_scale(x128):
    lane = lax.broadcasted_iota(jnp.int32, x128.shape, 1)
    first = lane < HEAD_DIM
    sq = x128 * x128
    s_a = jnp.sum(jnp.where(first, sq, 0.0), axis=-1, keepdims=True)
    s_b = jnp.sum(jnp.where(first, 0.0, sq), axis=-1, keepdims=True)
    r_a = lax.rsqrt(s_a * (1.0 / HEAD_DIM) + EPS)
    r_b = lax.rsqrt(s_b * (1.0 / HEAD_DIM) + EPS)
    return jnp.where(first, r_a, r_b)


def _rope(xn, cos, sin_hi, sin_lo):
    return (xn * cos
            + pltpu.roll(xn, ROPE_FREQS, 1) * sin_hi
            + pltpu.roll(xn, LANES - ROPE_FREQS, 1) * sin_lo)


def _softmax_rows(s):
    m = jnp.max(s, axis=-1, keepdims=True)
    p = jnp.exp(s - m)
    return p, jnp.sum(p, axis=-1, keepdims=True)


def _mem_kv_kernel(mem_ref, g_ref, w_ref, xk_ref, xv_ref):
    m = mem_ref[...]
    mn = (m * _rms_scale(m) * g_ref[...]).astype(BF16)
    kv = jnp.dot(mn, w_ref[...], preferred_element_type=F32)
    xk_ref[...] = kv[:, :D_MODEL].astype(BF16)
    xv_ref[...] = kv[:, D_MODEL:].astype(BF16)


def _mem_kv(mem2d, g_mem, w_xkv):
    rows = mem2d.shape[0]
    tm = 1024
    const = lambda i: (0, 0)
    return pl.pallas_call(
        _mem_kv_kernel,
        grid=(rows // tm,),
        in_specs=[
            pl.BlockSpec((tm, D_MODEL), lambda i: (i, 0)),
            pl.BlockSpec((1, D_MODEL), const),
            pl.BlockSpec((D_MODEL, 2 * D_MODEL), const),
        ],
        out_specs=[
            pl.BlockSpec((tm, D_MODEL), lambda i: (i, 0)),
            pl.BlockSpec((tm, D_MODEL), lambda i: (i, 0)),
        ],
        out_shape=[jax.ShapeDtypeStruct((rows, D_MODEL), BF16)] * 2,
        compiler_params=pltpu.CompilerParams(
            dimension_semantics=("arbitrary",), vmem_limit_bytes=VMEM_LIMIT),
        name="mem_kv",
    )(mem2d, g_mem, w_xkv)


def _in_proj_kernel(x_ref, gmix_ref, w_ref, bg_ref, gq_ref, gk_ref, qshift_ref,
                    cos_ref, shi_ref, slo_ref, q_ref, k_ref, v_ref, u_ref, gate_ref):
    x = x_ref[...]
    n1 = (x * _rms_scale(x) * gmix_ref[...]).astype(BF16)
    cos, shi, slo = cos_ref[...], shi_ref[...], slo_ref[...]
    lane = lax.broadcasted_iota(jnp.int32, (x.shape[0], LANES), 1)
    first = lane < HEAD_DIM

    def proj(c0, width):
        return jnp.dot(n1, w_ref[:, c0:c0 + width], preferred_element_type=F32)

    q = proj(_Q0, ATTN_WIDTH)
    gq = gq_ref[...]
    qshift = qshift_ref[...]
    for t in range(ATTN_WIDTH // LANES):
        qt = q[:, t * LANES:(t + 1) * LANES]
        qt = _rope(qt * _head_rms_scale(qt) * gq, cos, shi, slo)
        q_ref[:, (2 * t) * LANES:(2 * t + 1) * LANES] = jnp.where(first, qt, qshift).astype(BF16)
        q_ref[:, (2 * t + 1) * LANES:(2 * t + 2) * LANES] = jnp.where(
            first, pltpu.roll(qt, HEAD_DIM, 1), qshift).astype(BF16)

    kv = proj(_K0, 2 * KV_WIDTH)
    kt = kv[:, :KV_WIDTH]
    kt = _rope(kt * _head_rms_scale(kt) * gk_ref[...], cos, shi, slo)
    one_at_head_dim = jnp.where(lane == HEAD_DIM, 1.0, 0.0)
    k_ref[:, :LANES] = jnp.where(first, kt, one_at_head_dim).astype(BF16)
    k_ref[:, LANES:] = jnp.where(first, pltpu.roll(kt, HEAD_DIM, 1), one_at_head_dim).astype(BF16)
    v_ref[:, :LANES] = kv[:, KV_WIDTH:].astype(BF16)
    v_ref[:, LANES:] = jnp.where(lane == 0, 1.0, 0.0).astype(BF16)

    chunk = 512
    for c in range(2 * D_MODEL // chunk):
        g = proj(_G0 + c * chunk, chunk) + bg_ref[:, c * chunk:(c + 1) * chunk]
        gate_ref[:, c * chunk:(c + 1) * chunk] = jax.nn.sigmoid(g).astype(BF16)

    u_ref[...] = proj(_U0, POOL_WIDTH)


def _in_proj(x2d, g_mix, w_in, b_gate, gq_t, gk_t, qshift, cos_t, shi_t, slo_t, seq):
    rows = x2d.shape[0]
    tm = 512
    seq_tiles = seq // tm
    const = lambda i: (0, 0)
    row = lambda i: (i, 0)
    pos = lambda i: (i % seq_tiles, 0)
    in_width = w_in.shape[1]
    return pl.pallas_call(
        _in_proj_kernel,
        grid=(rows // tm,),
        in_specs=[
            pl.BlockSpec((tm, D_MODEL), row),
            pl.BlockSpec((1, D_MODEL), const),
            pl.BlockSpec((D_MODEL, in_width), const),
            pl.BlockSpec((1, 2 * D_MODEL), const),
            pl.BlockSpec((1, LANES), const),
            pl.BlockSpec((1, LANES), const),
            pl.BlockSpec((1, LANES), const),
            pl.BlockSpec((tm, LANES), pos),
            pl.BlockSpec((tm, LANES), pos),
            pl.BlockSpec((tm, LANES), pos),
        ],
        out_specs=[
            pl.BlockSpec((tm, N_Q_HEADS * LANES), row),
            pl.BlockSpec((tm, N_KV_HEADS * LANES), row),
            pl.BlockSpec((tm, 2 * LANES), row),
            pl.BlockSpec((tm, POOL_WIDTH), row),
            pl.BlockSpec((tm, 2 * D_MODEL), row),
        ],
        out_shape=[
            jax.ShapeDtypeStruct((rows, N_Q_HEADS * LANES), BF16),
            jax.ShapeDtypeStruct((rows, N_KV_HEADS * LANES), BF16),
            jax.ShapeDtypeStruct((rows, 2 * LANES), BF16),
            jax.ShapeDtypeStruct((rows, POOL_WIDTH), F32),
            jax.ShapeDtypeStruct((rows, 2 * D_MODEL), BF16),
        ],
        compiler_params=pltpu.CompilerParams(
            dimension_semantics=("arbitrary",), vmem_limit_bytes=VMEM_LIMIT),
        name="in_proj",
    )(x2d, g_mix, w_in, b_gate, gq_t, gk_t, qshift, cos_t, shi_t, slo_t)


def _mixer_kernel(q_ref, k_ref, v_ref, u_ref, uprev_ref, unext_ref, gate_ref, x_ref,
                  wau_ref, pw_ref, ps_ref, wpu_ref, wo_ref, h_ref, slab_ref, attn_ref,
                  *, tq, seq, subtract_max):
    step = pl.program_id(0)
    n_tiles = seq // tq
    i = jnp.maximum(step - 1, 0) % n_tiles

    @pl.when(step == 0)
    def _():
        attn_ref[...] = jnp.zeros_like(attn_ref)

    o_prev = attn_ref[...]

    v = v_ref[...]
    lane = lax.broadcasted_iota(jnp.int32, (tq, LANES), 1)
    first = lane < HEAD_DIM
    heads = []
    for h in range(N_Q_HEADS):
        g = h // (N_Q_HEADS // N_KV_HEADS)
        s = lax.dot_general(q_ref[:, h * LANES:(h + 1) * LANES], k_ref[:, g * LANES:(g + 1) * LANES],
                            (((1,), (1,)), ((), ())), preferred_element_type=F32)
        if subtract_max:
            s = s - jnp.max(s, axis=-1, keepdims=True)
        p = jnp.exp2(s).astype(BF16)
        r = jnp.dot(p, v, preferred_element_type=F32)
        heads.append(r[:, :LANES] / r[:, LANES:LANES + 1])
    half = N_Q_HEADS // 2
    attn_ref[...] = jnp.concatenate(
        [jnp.where(first, heads[t], heads[half + t]) for t in range(half)], axis=1).astype(BF16)
    a = jnp.dot(o_prev, wau_ref[...], preferred_element_type=F32)

    zero_halo = jnp.zeros((POOL_HALO, POOL_WIDTH), F32)
    slab_ref[0:POOL_HALO, :] = jnp.where(i > 0, uprev_ref[...], zero_halo)
    slab_ref[POOL_HALO:POOL_HALO + tq, :] = u_ref[...]
    slab_ref[POOL_HALO + tq:, :] = jnp.where(i < n_tiles - 1, unext_ref[...], zero_halo)
    t_pos = i * tq + lax.broadcasted_iota(jnp.int32, (tq, 1), 0)
    pm = []
    for gi, w in enumerate(POOL_WINDOWS):
        cols = slice(gi * POOL_GROUP, (gi + 1) * POOL_GROUP)
        win = None
        for j in range(-(w // 2), w - w // 2):
            piece = slab_ref[POOL_HALO + j:POOL_HALO + j + tq, cols]
            win = piece if win is None else win + piece
        lo = jnp.maximum(t_pos - w // 2, 0)
        hi = jnp.minimum(t_pos + (w - w // 2), seq)
        cnt = (hi - lo).astype(F32)
        pooled = win / cnt - u_ref[:, cols]
        pm.append(jnp.dot(pooled.astype(BF16), pw_ref[gi], preferred_element_type=F32))
    pm = jnp.concatenate(pm, axis=1) * ps_ref[...]
    p_branch = jnp.dot(pm.astype(BF16), wpu_ref[...], preferred_element_type=F32)

    g_a = gate_ref[:, :D_MODEL].astype(F32)
    g_p = gate_ref[:, D_MODEL:].astype(F32)
    mixed = (g_a * a + g_p * p_branch).astype(BF16)
    h_ref[...] = x_ref[...] + jnp.dot(mixed, wo_ref[...], preferred_element_type=F32)


def _mixer(q, k, v, u, gates, x2d, wau, pw, ps, wpu, wo, *, batch, seq, subtract_max):
    tq = 256
    n_tiles = seq // tq
    total = batch * n_tiles
    halo_blocks = tq // POOL_HALO
    seq_halos = seq // POOL_HALO
    attn_tile = lambda s: jnp.minimum(s, total - 1)
    post_tile = lambda s: jnp.maximum(s - 1, 0)
    attn_row = lambda s: (attn_tile(s), 0)
    per_batch = lambda s: (attn_tile(s) // n_tiles, 0)
    row = lambda s: (post_tile(s), 0)
    const2 = lambda s: (0, 0)
    const3 = lambda s: (0, 0, 0)

    def prev(s):
        t = post_tile(s)
        return ((t // n_tiles) * seq_halos + jnp.maximum((t % n_tiles) * halo_blocks - 1, 0), 0)

    def nxt(s):
        t = post_tile(s)
        return ((t // n_tiles) * seq_halos
                + jnp.minimum((t % n_tiles + 1) * halo_blocks, seq_halos - 1), 0)

    return pl.pallas_call(
        functools.partial(_mixer_kernel, tq=tq, seq=seq, subtract_max=subtract_max),
        grid=(total + 1,),
        in_specs=[
            pl.BlockSpec((tq, N_Q_HEADS * LANES), attn_row),
            pl.BlockSpec((seq, N_KV_HEADS * LANES), per_batch),
            pl.BlockSpec((seq, 2 * LANES), per_batch),
            pl.BlockSpec((tq, POOL_WIDTH), row),
            pl.BlockSpec((POOL_HALO, POOL_WIDTH), prev),
            pl.BlockSpec((POOL_HALO, POOL_WIDTH), nxt),
            pl.BlockSpec((tq, 2 * D_MODEL), row),
            pl.BlockSpec((tq, D_MODEL), row),
            pl.BlockSpec((ATTN_WIDTH, D_MODEL), const2),
            pl.BlockSpec((len(POOL_WINDOWS), POOL_GROUP, POOL_GROUP), const3),
            pl.BlockSpec((1, POOL_WIDTH), const2),
            pl.BlockSpec((POOL_WIDTH, D_MODEL), const2),
            pl.BlockSpec((D_MODEL, D_MODEL), const2),
        ],
        out_specs=pl.BlockSpec((tq, D_MODEL), row),
        out_shape=jax.ShapeDtypeStruct(x2d.shape, F32),
        scratch_shapes=[pltpu.VMEM((tq + 2 * POOL_HALO, POOL_WIDTH), F32),
                        pltpu.VMEM((tq, ATTN_WIDTH), BF16)],
        compiler_params=pltpu.CompilerParams(
            dimension_semantics=("arbitrary",), vmem_limit_bytes=VMEM_LIMIT),
        name="mixer_max" if subtract_max else "mixer",
    )(q, k, v, u, u, u, gates, x2d, wau, pw, ps, wpu, wo)


def _cross_ffn_kernel(h_ref, xk_ref, xv_ref, gc_ref, wq_ref, wo_ref, gf_ref, w1_ref, w2_ref,
                      gfin_ref, out_ref):
    h = h_ref[...]
    n2 = (h * _rms_scale(h) * gc_ref[...]).astype(BF16)
    xq = jnp.dot(n2, wq_ref[...], preferred_element_type=F32) * (X_HEAD_DIM ** -0.5)
    xq = xq.astype(BF16)
    outs = []
    for hd in range(N_X_HEADS):
        cols = slice(hd * X_HEAD_DIM, (hd + 1) * X_HEAD_DIM)
        s = lax.dot_general(xq[:, cols], xk_ref[:, cols], (((1,), (1,)), ((), ())),
                            preferred_element_type=F32)
        p, l = _softmax_rows(s)
        o = jnp.dot(p.astype(BF16), xv_ref[:, cols], preferred_element_type=F32)
        outs.append((o / l).astype(BF16))
    xo = jnp.concatenate(outs, axis=1)
    h = h + jnp.dot(xo, wo_ref[...], preferred_element_type=F32)

    n3 = (h * _rms_scale(h) * gf_ref[...]).astype(BF16)
    chunk = 1024
    acc = h
    for c in range(D_FF // chunk):
        t = jnp.dot(n3, w1_ref[:, c * chunk:(c + 1) * chunk], preferred_element_type=F32)
        t = jnp.square(jnp.maximum(t, 0.0)).astype(BF16)
        acc = acc + jnp.dot(t, w2_ref[c * chunk:(c + 1) * chunk, :], preferred_element_type=F32)
    out_ref[...] = acc * _rms_scale(acc) * gfin_ref[...]


def _cross_ffn(h2d, xk, xv, g_cross, w_xq, w_xo, g_ffn, w_ff1, w_ff2, g_final, seq, n_mem):
    rows = h2d.shape[0]
    tm = 512
    tiles_per_batch = seq // tm
    const = lambda i: (0, 0)
    row = lambda i: (i, 0)
    per_batch = lambda i: (i // tiles_per_batch, 0)
    single = pl.Buffered(1)
    return pl.pallas_call(
        _cross_ffn_kernel,
        grid=(rows // tm,),
        in_specs=[
            pl.BlockSpec((tm, D_MODEL), row),
            pl.BlockSpec((n_mem, D_MODEL), per_batch),
            pl.BlockSpec((n_mem, D_MODEL), per_batch),
            pl.BlockSpec((1, D_MODEL), const),
            pl.BlockSpec((D_MODEL, D_MODEL), const, pipeline_mode=single),
            pl.BlockSpec((D_MODEL, D_MODEL), const, pipeline_mode=single),
            pl.BlockSpec((1, D_MODEL), const),
            pl.BlockSpec((D_MODEL, D_FF), const, pipeline_mode=single),
            pl.BlockSpec((D_FF, D_MODEL), const, pipeline_mode=single),
            pl.BlockSpec((1, D_MODEL), const),
        ],
        out_specs=pl.BlockSpec((tm, D_MODEL), row),
        out_shape=jax.ShapeDtypeStruct(h2d.shape, F32),
        compiler_params=pltpu.CompilerParams(
            dimension_semantics=("arbitrary",), vmem_limit_bytes=VMEM_LIMIT),
        name="cross_ffn",
    )(h2d, xk, xv, g_cross, w_xq, w_xo, g_ffn, w_ff1, w_ff2, g_final)


def _rope_tables(seq):
    t = lax.broadcasted_iota(jnp.int32, (seq, LANES), 0)
    d = lax.broadcasted_iota(jnp.int32, (seq, LANES), 1) % HEAD_DIM
    freq = d % ROPE_FREQS
    upper_half = (d // ROPE_FREQS) % 2 == 1
    pos = jnp.where(d < HEAD_DIM // 2, t // GRID_W, t % GRID_W).astype(F32)
    inv = ROPE_THETA ** (-(2 * freq).astype(F32) / (2 * ROPE_FREQS))
    ang = pos * inv
    cos, sin = jnp.cos(ang), jnp.sin(ang)
    return cos, jnp.where(upper_half, sin, 0.0), jnp.where(upper_half, 0.0, -sin)


def kernel(x, mem, g_mix, w_in, b_gate, g_q, g_k, w_attn_up, pool_w, pool_scale, w_pool_up, w_out,
           g_cross, g_mem, w_xq, w_xkv, w_xo, g_ffn, w_ff1, w_ff2, g_final):
    batch, seq, d = x.shape
    n_mem = mem.shape[1]
    assert d == D_MODEL and w_in.shape[0] == 1 and seq % 512 == 0 and seq % GRID_W == 0
    l = 0
    x2d = x.reshape(batch * seq, d)
    mem2d = mem.reshape(batch * n_mem, d)
    row = lambda a: a.reshape(1, -1)

    cos_t, shi_t, slo_t = _rope_tables(seq)
    two_heads = lambda g: jnp.concatenate([g, g]).reshape(1, LANES)
    gq_t = two_heads(g_q[l] * (HEAD_DIM ** -0.5 * LOG2_E))
    gk_t = two_heads(g_k[l])
    shift = 1.02 * HEAD_DIM ** 0.5 * jnp.max(jnp.abs(g_q[l])) * jnp.max(jnp.abs(g_k[l])) * LOG2_E
    fixed_shift_ok = shift <= MAX_FIXED_SHIFT * LOG2_E
    qshift = jnp.where((jnp.arange(LANES) == HEAD_DIM) & fixed_shift_ok, -shift, 0.0)
    qshift = qshift.astype(F32).reshape(1, LANES)
    half = N_Q_HEADS // 2
    wau = (w_attn_up[l].reshape(2, half, HEAD_DIM, d).transpose(1, 0, 2, 3)
           .reshape(ATTN_WIDTH, d).astype(BF16))

    xk, xv = _mem_kv(mem2d, row(g_mem[l]), w_xkv[l].astype(BF16))
    q, k, v, u, gates = _in_proj(x2d, row(g_mix[l]), w_in[l].astype(BF16), row(b_gate[l]),
                                 gq_t, gk_t, qshift, cos_t, shi_t, slo_t, seq)
    mixer_args = (q, k, v, u, gates, x2d, wau, pool_w[l].astype(BF16), row(pool_scale[l]),
                  w_pool_up[l].astype(BF16), w_out[l].astype(BF16))
    h1 = lax.cond(
        fixed_shift_ok,
        lambda *a: _mixer(*a, batch=batch, seq=seq, subtract_max=False),
        lambda *a: _mixer(*a, batch=batch, seq=seq, subtract_max=True),
        *mixer_args)
    out = _cross_ffn(h1, xk, xv, row(g_cross[l]), w_xq[l].astype(BF16), w_xo[l].astype(BF16),
                     row(g_ffn[l]), w_ff1[l].astype(BF16), w_ff2[l].astype(BF16), row(g_final),
                     seq, n_mem)
    return out.reshape(batch, seq, d)
```

```python
import functools
import math

import jax
import jax.numpy as jnp
from jax import lax
from jax.experimental import pallas as pl
from jax.experimental.pallas import tpu as pltpu

D_MODEL = 1024
GRID_W = 64
HEAD_DIM = 64
N_Q_HEADS = 8
N_KV_HEADS = 2
ATTN_WIDTH = N_Q_HEADS * HEAD_DIM
KV_WIDTH = N_KV_HEADS * HEAD_DIM
POOL_WINDOWS = (2, 4, 8, 16)
POOL_WIDTH = 512
POOL_GROUP = 128
ROPE_THETA = 10000.0
ROPE_FREQS = 16
N_X_HEADS = 4
X_HEAD_DIM = 256
D_FF = 4096
EPS = 1e-6

LANES = 128
POOL_HALO = 8
VMEM_LIMIT = 56 * 1024 * 1024
LOG2_E = math.log2(math.e)
MAX_FIXED_SHIFT = 40.0

BF16 = jnp.bfloat16
F32 = jnp.float32

_Q0, _K0, _V0, _U0, _G0 = 0, 512, 640, 768, 1280


def _rms_scale(x):
    return lax.rsqrt(jnp.mean(x * x, axis=-1, keepdims=True) + EPS)


def _head_rms_scale(x128):
    lane = lax.broadcasted_iota(jnp.int32, x128.shape, 1)
    first = lane < HEAD_DIM
    sq = x128 * x128
    s_a = jnp.sum(jnp.where(first, sq, 0.0), axis=-1, keepdims=True)
    s_b = jnp.sum(jnp.where(first, 0.0, sq), axis=-1, keepdims=True)
    r_a = lax.rsqrt(s_a * (1.0 / HEAD_DIM) + EPS)
    r_b = lax.rsqrt(s_b * (1.0 / HEAD_DIM) + EPS)
    return jnp.where(first, r_a, r_b)


def _rope(xn, cos, sin_hi, sin_lo):
    return (xn * cos
            + pltpu.roll(xn, ROPE_FREQS, 1) * sin_hi
            + pltpu.roll(xn, LANES - ROPE_FREQS, 1) * sin_lo)


def _softmax_rows(s):
    m = jnp.max(s, axis=-1, keepdims=True)
    p = jnp.exp(s - m)
    return p, jnp.sum(p, axis=-1, keepdims=True)


def _mem_kv_kernel(mem_ref, g_ref, w_ref, xk_ref, xv_ref):
    m = mem_ref[...]
    mn = (m * _rms_scale(m) * g_ref[...]).astype(BF16)
    kv = jnp.dot(mn, w_ref[...], preferred_element_type=F32)
    xk_ref[...] = kv[:, :D_MODEL].astype(BF16)
    xv_ref[...] = kv[:, D_MODEL:].astype(BF16)


def _mem_kv(mem2d, g_mem, w_xkv):
    rows = mem2d.shape[0]
    tm = 1024
    const = lambda i: (0, 0)
    return pl.pallas_call(
        _mem_kv_kernel,
        grid=(rows // tm,),
        in_specs=[
            pl.BlockSpec((tm, D_MODEL), lambda i: (i, 0)),
            pl.BlockSpec((1, D_MODEL), const),
            pl.BlockSpec((D_MODEL, 2 * D_MODEL), const),
        ],
        out_specs=[
            pl.BlockSpec((tm, D_MODEL), lambda i: (i, 0)),
            pl.BlockSpec((tm, D_MODEL), lambda i: (i, 0)),
        ],
        out_shape=[jax.ShapeDtypeStruct((rows, D_MODEL), BF16)] * 2,
        compiler_params=pltpu.CompilerParams(
            dimension_semantics=("arbitrary",), vmem_limit_bytes=VMEM_LIMIT),
        name="mem_kv",
    )(mem2d, g_mem, w_xkv)


def _in_proj_kernel(x_ref, gmix_ref, w_ref, bg_ref, gq_ref, gk_ref, qshift_ref,
                    cos_ref, shi_ref, slo_ref, q_ref, k_ref, v_ref, u_ref, gate_ref):
    x = x_ref[...]
    n1 = (x * _rms_scale(x) * gmix_ref[...]).astype(BF16)
    cos, shi, slo = cos_ref[...], shi_ref[...], slo_ref[...]
    lane = lax.broadcasted_iota(jnp.int32, (x.shape[0], LANES), 1)
    first = lane < HEAD_DIM

    def proj(c0, width):
        return jnp.dot(n1, w_ref[:, c0:c0 + width], preferred_element_type=F32)

    q = proj(_Q0, ATTN_WIDTH)
    gq = gq_ref[...]
    qshift = qshift_ref[...]
    for t in range(ATTN_WIDTH // LANES):
        qt = q[:, t * LANES:(t + 1) * LANES]
        qt = _rope(qt * _head_rms_scale(qt) * gq, cos, shi, slo)
        q_ref[:, (2 * t) * LANES:(2 * t + 1) * LANES] = jnp.where(first, qt, qshift).astype(BF16)
        q_ref[:, (2 * t + 1) * LANES:(2 * t + 2) * LANES] = jnp.where(
            first, pltpu.roll(qt, HEAD_DIM, 1), qshift).astype(BF16)

    kv = proj(_K0, 2 * KV_WIDTH)
    kt = kv[:, :KV_WIDTH]
    kt = _rope(kt * _head_rms_scale(kt) * gk_ref[...], cos, shi, slo)
    one_at_head_dim = jnp.where(lane == HEAD_DIM, 1.0, 0.0)
    k_ref[:, :LANES] = jnp.where(first, kt, one_at_head_dim).astype(BF16)
    k_ref[:, LANES:] = jnp.where(first, pltpu.roll(kt, HEAD_DIM, 1), one_at_head_dim).astype(BF16)
    v_ref[:, :LANES] = kv[:, KV_WIDTH:].astype(BF16)
    v_ref[:, LANES:] = jnp.where(lane == 0, 1.0, 0.0).astype(BF16)

    chunk = 512
    for c in range(2 * D_MODEL // chunk):
        g = proj(_G0 + c * chunk, chunk) + bg_ref[:, c * chunk:(c + 1) * chunk]
        gate_ref[:, c * chunk:(c + 1) * chunk] = jax.nn.sigmoid(g).astype(BF16)

    u_ref[...] = proj(_U0, POOL_WIDTH)


def _in_proj(x2d, g_mix, w_in, b_gate, gq_t, gk_t, qshift, cos_t, shi_t, slo_t, seq):
    rows = x2d.shape[0]
    tm = 512
    seq_tiles = seq // tm
    const = lambda i: (0, 0)
    row = lambda i: (i, 0)
    pos = lambda i: (i % seq_tiles, 0)
    in_width = w_in.shape[1]
    return pl.pallas_call(
        _in_proj_kernel,
        grid=(rows // tm,),
        in_specs=[
            pl.BlockSpec((tm, D_MODEL), row),
            pl.BlockSpec((1, D_MODEL), const),
            pl.BlockSpec((D_MODEL, in_width), const),
            pl.BlockSpec((1, 2 * D_MODEL), const),
            pl.BlockSpec((1, LANES), const),
            pl.BlockSpec((1, LANES), const),
            pl.BlockSpec((1, LANES), const),
            pl.BlockSpec((tm, LANES), pos),
            pl.BlockSpec((tm, LANES), pos),
            pl.BlockSpec((tm, LANES), pos),
        ],
        out_specs=[
            pl.BlockSpec((tm, N_Q_HEADS * LANES), row),
            pl.BlockSpec((tm, N_KV_HEADS * LANES), row),
            pl.BlockSpec((tm, 2 * LANES), row),
            pl.BlockSpec((tm, POOL_WIDTH), row),
            pl.BlockSpec((tm, 2 * D_MODEL), row),
        ],
        out_shape=[
            jax.ShapeDtypeStruct((rows, N_Q_HEADS * LANES), BF16),
            jax.ShapeDtypeStruct((rows, N_KV_HEADS * LANES), BF16),
            jax.ShapeDtypeStruct((rows, 2 * LANES), BF16),
            jax.ShapeDtypeStruct((rows, POOL_WIDTH), F32),
            jax.ShapeDtypeStruct((rows, 2 * D_MODEL), BF16),
        ],
        compiler_params=pltpu.CompilerParams(
            dimension_semantics=("arbitrary",), vmem_limit_bytes=VMEM_LIMIT),
        name="in_proj",
    )(x2d, g_mix, w_in, b_gate, gq_t, gk_t, qshift, cos_t, shi_t, slo_t)


def _pooled(i, n_tiles, u_ref, uprev_ref, unext_ref, icnt_ref, slab_ref, wa_ref, wb_ref):
    tq = u_ref.shape[0]
    span = tq + 2 * POOL_HALO
    zero_halo = jnp.zeros((POOL_HALO, POOL_WIDTH), F32)
    slab_ref[0:POOL_HALO, :] = jnp.where(i > 0, uprev_ref[...], zero_halo)
    slab_ref[POOL_HALO:POOL_HALO + tq, :] = u_ref[...]
    slab_ref[POOL_HALO + tq:span, :] = jnp.where(i < n_tiles - 1, unext_ref[...], zero_halo)
    slab_ref[span:, :] = zero_halo
    wa_ref[span:, :] = jnp.zeros((POOL_HALO, 2 * POOL_GROUP), F32)
    wb_ref[span:, :] = jnp.zeros((POOL_HALO, 2 * POOL_GROUP), F32)

    g0 = slice(0, POOL_GROUP)
    g1 = slice(POOL_GROUP, 2 * POOL_GROUP)
    wide = slice(2 * POOL_GROUP, POOL_WIDTH)
    h = POOL_HALO
    win2 = slab_ref[h - 1:h - 1 + tq, g0] + slab_ref[h:h + tq, g0]
    win4 = ((slab_ref[h - 2:h - 2 + tq, g1] + slab_ref[h - 1:h - 1 + tq, g1])
            + (slab_ref[h:h + tq, g1] + slab_ref[h + 1:h + 1 + tq, g1]))
    wa_ref[0:span, :] = slab_ref[0:span, wide] + slab_ref[1:span + 1, wide]
    wb_ref[0:span, :] = wa_ref[0:span, :] + wa_ref[2:span + 2, :]
    wa_ref[0:span, :] = wb_ref[0:span, :] + wb_ref[4:span + 4, :]
    win8 = wa_ref[h - 4:h - 4 + tq, 0:POOL_GROUP]
    win16 = wa_ref[0:tq, POOL_GROUP:] + wa_ref[h:h + tq, POOL_GROUP:]

    pooled = []
    for gi, win in enumerate((win2, win4, win8, win16)):
        cols = slice(gi * POOL_GROUP, (gi + 1) * POOL_GROUP)
        pooled.append((win * icnt_ref[:, cols] - u_ref[:, cols]).astype(BF16))
    return pooled


def _mixer_kernel(q_ref, k_ref, v_ref, u_ref, uprev_ref, unext_ref, icnt_ref, gate_ref, x_ref,
                  wau_ref, pw_ref, ps_ref, wpu_ref, wo_ref, h_ref, slab_ref, wa_ref, wb_ref,
                  attn_ref, *, tq, seq, subtract_max):
    step = pl.program_id(0)
    n_tiles = seq // tq
    i = jnp.maximum(step - 1, 0) % n_tiles

    @pl.when(step == 0)
    def _():
        attn_ref[...] = jnp.zeros_like(attn_ref)

    o_prev = attn_ref[...]

    pooled = _pooled(i, n_tiles, u_ref, uprev_ref, unext_ref, icnt_ref, slab_ref, wa_ref, wb_ref)

    v = v_ref[...]
    lane = lax.broadcasted_iota(jnp.int32, (tq, LANES), 1)
    first = lane < HEAD_DIM
    heads = []
    for h in range(N_Q_HEADS):
        g = h // (N_Q_HEADS // N_KV_HEADS)
        s = lax.dot_general(q_ref[:, h * LANES:(h + 1) * LANES], k_ref[:, g * LANES:(g + 1) * LANES],
                            (((1,), (1,)), ((), ())), preferred_element_type=F32)
        if subtract_max:
            s = s - jnp.max(s, axis=-1, keepdims=True)
        p = jnp.exp2(s).astype(BF16)
        r = jnp.dot(p, v, preferred_element_type=F32)
        heads.append(r[:, :LANES] / r[:, LANES:LANES + 1])
    half = N_Q_HEADS // 2
    attn_ref[...] = jnp.concatenate(
        [jnp.where(first, heads[t], heads[half + t]) for t in range(half)], axis=1).astype(BF16)
    a = jnp.dot(o_prev, wau_ref[...], preferred_element_type=F32)

    pm = [jnp.dot(pooled[gi], pw_ref[gi], preferred_element_type=F32)
          for gi in range(len(POOL_WINDOWS))]
    pm = jnp.concatenate(pm, axis=1) * ps_ref[...]
    p_branch = jnp.dot(pm.astype(BF16), wpu_ref[...], preferred_element_type=F32)

    g_a = gate_ref[:, :D_MODEL].astype(F32)
    g_p = gate_ref[:, D_MODEL:].astype(F32)
    mixed = (g_a * a + g_p * p_branch).astype(BF16)
    h_ref[...] = x_ref[...] + jnp.dot(mixed, wo_ref[...], preferred_element_type=F32)


def _mixer(q, k, v, u, icnt, gates, x2d, wau, pw, ps, wpu, wo, *, batch, seq, subtract_max):
    tq = 256
    n_tiles = seq // tq
    total = batch * n_tiles
    halo_blocks = tq // POOL_HALO
    seq_halos = seq // POOL_HALO
    attn_tile = lambda s: jnp.minimum(s, total - 1)
    post_tile = lambda s: jnp.maximum(s - 1, 0)
    attn_row = lambda s: (attn_tile(s), 0)
    per_batch = lambda s: (attn_tile(s) // n_tiles, 0)
    row = lambda s: (post_tile(s), 0)
    seq_row = lambda s: (post_tile(s) % n_tiles, 0)
    const2 = lambda s: (0, 0)
    const3 = lambda s: (0, 0, 0)

    def prev(s):
        t = post_tile(s)
        return ((t // n_tiles) * seq_halos + jnp.maximum((t % n_tiles) * halo_blocks - 1, 0), 0)

    def nxt(s):
        t = post_tile(s)
        return ((t // n_tiles) * seq_halos
                + jnp.minimum((t % n_tiles + 1) * halo_blocks, seq_halos - 1), 0)

    return pl.pallas_call(
        functools.partial(_mixer_kernel, tq=tq, seq=seq, subtract_max=subtract_max),
        grid=(total + 1,),
        in_specs=[
            pl.BlockSpec((tq, N_Q_HEADS * LANES), attn_row),
            pl.BlockSpec((seq, N_KV_HEADS * LANES), per_batch),
            pl.BlockSpec((seq, 2 * LANES), per_batch),
            pl.BlockSpec((tq, POOL_WIDTH), row),
            pl.BlockSpec((POOL_HALO, POOL_WIDTH), prev),
            pl.BlockSpec((POOL_HALO, POOL_WIDTH), nxt),
            pl.BlockSpec((tq, POOL_WIDTH), seq_row),
            pl.BlockSpec((tq, 2 * D_MODEL), row),
            pl.BlockSpec((tq, D_MODEL), row),
            pl.BlockSpec((ATTN_WIDTH, D_MODEL), const2),
            pl.BlockSpec((len(POOL_WINDOWS), POOL_GROUP, POOL_GROUP), const3),
            pl.BlockSpec((1, POOL_WIDTH), const2),
            pl.BlockSpec((POOL_WIDTH, D_MODEL), const2),
            pl.BlockSpec((D_MODEL, D_MODEL), const2),
        ],
        out_specs=pl.BlockSpec((tq, D_MODEL), row),
        out_shape=jax.ShapeDtypeStruct(x2d.shape, F32),
        scratch_shapes=[pltpu.VMEM((tq + 3 * POOL_HALO, POOL_WIDTH), F32),
                        pltpu.VMEM((tq + 3 * POOL_HALO, 2 * POOL_GROUP), F32),
                        pltpu.VMEM((tq + 3 * POOL_HALO, 2 * POOL_GROUP), F32),
                        pltpu.VMEM((tq, ATTN_WIDTH), BF16)],
        compiler_params=pltpu.CompilerParams(
            dimension_semantics=("arbitrary",), vmem_limit_bytes=VMEM_LIMIT),
        name="mixer_max" if subtract_max else "mixer",
    )(q, k, v, u, u, u, icnt, gates, x2d, wau, pw, ps, wpu, wo)


def _cross_ffn_kernel(h_ref, xk_ref, xv_ref, gc_ref, wq_ref, wo_ref, gf_ref, w1_ref, w2_ref,
                      gfin_ref, out_ref, *, parts):
    rows = h_ref.shape[0] // parts
    sl = [slice(j * rows, (j + 1) * rows) for j in range(parts)]
    every = range(parts)

    h = [h_ref[sl[j], :] for j in every]
    n2 = [(h[j] * _rms_scale(h[j]) * gc_ref[...]).astype(BF16) for j in every]
    xq = [(jnp.dot(n2[j], wq_ref[...], preferred_element_type=F32) * (X_HEAD_DIM ** -0.5))
          .astype(BF16) for j in every]
    outs = [[] for _ in every]
    for hd in range(N_X_HEADS):
        cols = slice(hd * X_HEAD_DIM, (hd + 1) * X_HEAD_DIM)
        for j in every:
            s = lax.dot_general(xq[j][:, cols], xk_ref[:, cols], (((1,), (1,)), ((), ())),
                                preferred_element_type=F32)
            p, l = _softmax_rows(s)
            o = jnp.dot(p.astype(BF16), xv_ref[:, cols], preferred_element_type=F32)
            outs[j].append((o / l).astype(BF16))
    h = [h[j] + jnp.dot(jnp.concatenate(outs[j], axis=1), wo_ref[...], preferred_element_type=F32)
         for j in every]

    n3 = [(h[j] * _rms_scale(h[j]) * gf_ref[...]).astype(BF16) for j in every]
    chunk = 1024
    acc = h
    for c in range(D_FF // chunk):
        t = [jnp.dot(n3[j], w1_ref[:, c * chunk:(c + 1) * chunk], preferred_element_type=F32)
             for j in every]
        t = [jnp.square(jnp.maximum(t[j], 0.0)).astype(BF16) for j in every]
        acc = [acc[j] + jnp.dot(t[j], w2_ref[c * chunk:(c + 1) * chunk, :],
                                preferred_element_type=F32) for j in every]
    for j in every:
        out_ref[sl[j], :] = acc[j] * _rms_scale(acc[j]) * gfin_ref[...]


def _cross_ffn(h2d, xk, xv, g_cross, w_xq, w_xo, g_ffn, w_ff1, w_ff2, g_final, seq, n_mem):
    rows = h2d.shape[0]
    tm = 512
    tiles_per_batch = seq // tm
    const = lambda i: (0, 0)
    row = lambda i: (i, 0)
    per_batch = lambda i: (i // tiles_per_batch, 0)
    single = pl.Buffered(1)
    return pl.pallas_call(
        functools.partial(_cross_ffn_kernel, parts=2),
        grid=(rows // tm,),
        in_specs=[
            pl.BlockSpec((tm, D_MODEL), row),
            pl.BlockSpec((n_mem, D_MODEL), per_batch),
            pl.BlockSpec((n_mem, D_MODEL), per_batch),
            pl.BlockSpec((1, D_MODEL), const),
            pl.BlockSpec((D_MODEL, D_MODEL), const, pipeline_mode=single),
            pl.BlockSpec((D_MODEL, D_MODEL), const, pipeline_mode=single),
            pl.BlockSpec((1, D_MODEL), const),
            pl.BlockSpec((D_MODEL, D_FF), const, pipeline_mode=single),
            pl.BlockSpec((D_FF, D_MODEL), const, pipeline_mode=single),
            pl.BlockSpec((1, D_MODEL), const),
        ],
        out_specs=pl.BlockSpec((tm, D_MODEL), row),
        out_shape=jax.ShapeDtypeStruct(h2d.shape, F32),
        compiler_params=pltpu.CompilerParams(
            dimension_semantics=("arbitrary",), vmem_limit_bytes=VMEM_LIMIT),
        name="cross_ffn",
    )(h2d, xk, xv, g_cross, w_xq, w_xo, g_ffn, w_ff1, w_ff2, g_final)


def _rope_tables(seq):
    t = lax.broadcasted_iota(jnp.int32, (seq, LANES), 0)
    d = lax.broadcasted_iota(jnp.int32, (seq, LANES), 1) % HEAD_DIM
    freq = d % ROPE_FREQS
    upper_half = (d // ROPE_FREQS) % 2 == 1
    pos = jnp.where(d < HEAD_DIM // 2, t // GRID_W, t % GRID_W).astype(F32)
    inv = ROPE_THETA ** (-(2 * freq).astype(F32) / (2 * ROPE_FREQS))
    ang = pos * inv
    cos, sin = jnp.cos(ang), jnp.sin(ang)
    return cos, jnp.where(upper_half, sin, 0.0), jnp.where(upper_half, 0.0, -sin)


def _pool_inverse_counts(seq):
    t = lax.broadcasted_iota(jnp.int32, (seq, POOL_WIDTH), 0)
    group = lax.broadcasted_iota(jnp.int32, (seq, POOL_WIDTH), 1) // POOL_GROUP
    w = jnp.left_shift(POOL_WINDOWS[0], group)
    assert POOL_WINDOWS == tuple(POOL_WINDOWS[0] << g for g in range(len(POOL_WINDOWS)))
    lo = jnp.maximum(t - w // 2, 0)
    hi = jnp.minimum(t + (w - w // 2), seq)
    return 1.0 / (hi - lo).astype(F32)


def kernel(x, mem, g_mix, w_in, b_gate, g_q, g_k, w_attn_up, pool_w, pool_scale, w_pool_up, w_out,
           g_cross, g_mem, w_xq, w_xkv, w_xo, g_ffn, w_ff1, w_ff2, g_final):
    batch, seq, d = x.shape
    n_mem = mem.shape[1]
    assert d == D_MODEL and w_in.shape[0] == 1 and seq % 512 == 0 and seq % GRID_W == 0
    l = 0
    x2d = x.reshape(batch * seq, d)
    mem2d = mem.reshape(batch * n_mem, d)
    row = lambda a: a.reshape(1, -1)

    cos_t, shi_t, slo_t = _rope_tables(seq)
    two_heads = lambda g: jnp.concatenate([g, g]).reshape(1, LANES)
    gq_t = two_heads(g_q[l] * (HEAD_DIM ** -0.5 * LOG2_E))
    gk_t = two_heads(g_k[l])
    shift = 1.02 * HEAD_DIM ** 0.5 * jnp.max(jnp.abs(g_q[l])) * jnp.max(jnp.abs(g_k[l])) * LOG2_E
    fixed_shift_ok = shift <= MAX_FIXED_SHIFT * LOG2_E
    qshift = jnp.where((jnp.arange(LANES) == HEAD_DIM) & fixed_shift_ok, -shift, 0.0)
    qshift = qshift.astype(F32).reshape(1, LANES)
    half = N_Q_HEADS // 2
    wau = (w_attn_up[l].reshape(2, half, HEAD_DIM, d).transpose(1, 0, 2, 3)
           .reshape(ATTN_WIDTH, d).astype(BF16))

    xk, xv = _mem_kv(mem2d, row(g_mem[l]), w_xkv[l].astype(BF16))
    q, k, v, u, gates = _in_proj(x2d, row(g_mix[l]), w_in[l].astype(BF16), row(b_gate[l]),
                                 gq_t, gk_t, qshift, cos_t, shi_t, slo_t, seq)
    mixer_args = (q, k, v, u, _pool_inverse_counts(seq), gates, x2d, wau, pool_w[l].astype(BF16), row(pool_scale[l]),
                  w_pool_up[l].astype(BF16), w_out[l].astype(BF16))
    h1 = lax.cond(
        fixed_shift_ok,
        lambda *a: _mixer(*a, batch=batch, seq=seq, subtract_max=False),
        lambda *a: _mixer(*a, batch=batch, seq=seq, subtract_max=True),
        *mixer_args)
    out = _cross_ffn(h1, xk, xv, row(g_cross[l]), w_xq[l].astype(BF16), w_xo[l].astype(BF16),
                     row(g_ffn[l]), w_ff1[l].astype(BF16), w_ff2[l].astype(BF16), row(g_final),
                     seq, n_mem)
    return out.reshape(batch, seq, d)
```

```python
import functools
import math

import jax
import jax.numpy as jnp
from jax import lax
from jax.experimental import pallas as pl
from jax.experimental.pallas import tpu as pltpu

D_MODEL = 1024
GRID_W = 64
HEAD_DIM = 64
N_Q_HEADS = 8
N_KV_HEADS = 2
ATTN_WIDTH = N_Q_HEADS * HEAD_DIM
KV_WIDTH = N_KV_HEADS * HEAD_DIM
POOL_WINDOWS = (2, 4, 8, 16)
POOL_WIDTH = 512
POOL_GROUP = 128
ROPE_THETA = 10000.0
ROPE_FREQS = 16
N_X_HEADS = 4
X_HEAD_DIM = 256
D_FF = 4096
EPS = 1e-6

LANES = 128
POOL_HALO = 8
VMEM_LIMIT = 56 * 1024 * 1024
LOG2_E = math.log2(math.e)
MAX_FIXED_SHIFT = 40.0

BF16 = jnp.bfloat16
F32 = jnp.float32

_Q0, _K0, _V0, _U0, _G0 = 0, 512, 640, 768, 1280


def _rms_scale(x):
    return lax.rsqrt(jnp.mean(x * x, axis=-1, keepdims=True) + EPS)


def _head_rms_scale(x128):
    lane = lax.broadcasted_iota(jnp.int32, x128.shape, 1)
    first = lane < HEAD_DIM
    sq = x128 * x128
    s_a = jnp.sum(jnp.where(first, sq, 0.0), axis=-1, keepdims=True)
    s_b = jnp.sum(jnp.where(first, 0.0, sq), axis=-1, keepdims=True)
    r_a = lax.rsqrt(s_a * (1.0 / HEAD_DIM) + EPS)
    r_b = lax.rsqrt(s_b * (1.0 / HEAD_DIM) + EPS)
    return jnp.where(first, r_a, r_b)


def _rope(xn, cos, sin_hi, sin_lo):
    return (xn * cos
            + pltpu.roll(xn, ROPE_FREQS, 1) * sin_hi
            + pltpu.roll(xn, LANES - ROPE_FREQS, 1) * sin_lo)


def _softmax_rows(s):
    m = jnp.max(s, axis=-1, keepdims=True)
    p = jnp.exp(s - m)
    return p, jnp.sum(p, axis=-1, keepdims=True)


def _mem_kv_kernel(mem_ref, g_ref, w_ref, xk_ref, xv_ref):
    m = mem_ref[...]
    mn = (m * _rms_scale(m) * g_ref[...]).astype(BF16)
    kv = jnp.dot(mn, w_ref[...], preferred_element_type=F32)
    xk_ref[...] = kv[:, :D_MODEL].astype(BF16)
    xv_ref[...] = kv[:, D_MODEL:].astype(BF16)


def _mem_kv(mem2d, g_mem, w_xkv):
    rows = mem2d.shape[0]
    tm = 1024
    const = lambda i: (0, 0)
    return pl.pallas_call(
        _mem_kv_kernel,
        grid=(rows // tm,),
        in_specs=[
            pl.BlockSpec((tm, D_MODEL), lambda i: (i, 0)),
            pl.BlockSpec((1, D_MODEL), const),
            pl.BlockSpec((D_MODEL, 2 * D_MODEL), const),
        ],
        out_specs=[
            pl.BlockSpec((tm, D_MODEL), lambda i: (i, 0)),
            pl.BlockSpec((tm, D_MODEL), lambda i: (i, 0)),
        ],
        out_shape=[jax.ShapeDtypeStruct((rows, D_MODEL), BF16)] * 2,
        compiler_params=pltpu.CompilerParams(
            dimension_semantics=("arbitrary",), vmem_limit_bytes=VMEM_LIMIT),
        name="mem_kv",
    )(mem2d, g_mem, w_xkv)


def _in_proj_kernel(x_ref, gmix_ref, w_ref, bg_ref, gq_ref, gk_ref, qshift_ref,
                    cos_ref, shi_ref, slo_ref, q_ref, k_ref, v_ref, u_ref, gate_ref):
    x = x_ref[...]
    n1 = (x * _rms_scale(x) * gmix_ref[...]).astype(BF16)
    cos, shi, slo = cos_ref[...], shi_ref[...], slo_ref[...]
    lane = lax.broadcasted_iota(jnp.int32, (x.shape[0], LANES), 1)
    first = lane < HEAD_DIM

    def proj(c0, width):
        return jnp.dot(n1, w_ref[:, c0:c0 + width], preferred_element_type=F32)

    q = proj(_Q0, ATTN_WIDTH)
    gq = gq_ref[...]
    qshift = qshift_ref[...]
    for t in range(ATTN_WIDTH // LANES):
        qt = q[:, t * LANES:(t + 1) * LANES]
        qt = _rope(qt * _head_rms_scale(qt) * gq, cos, shi, slo)
        q_ref[:, (2 * t) * LANES:(2 * t + 1) * LANES] = jnp.where(first, qt, qshift).astype(BF16)
        q_ref[:, (2 * t + 1) * LANES:(2 * t + 2) * LANES] = jnp.where(
            first, pltpu.roll(qt, HEAD_DIM, 1), qshift).astype(BF16)

    kv = proj(_K0, 2 * KV_WIDTH)
    kt = kv[:, :KV_WIDTH]
    kt = _rope(kt * _head_rms_scale(kt) * gk_ref[...], cos, shi, slo)
    one_at_head_dim = jnp.where(lane == HEAD_DIM, 1.0, 0.0)
    k_ref[:, :LANES] = jnp.where(first, kt, one_at_head_dim).astype(BF16)
    k_ref[:, LANES:] = jnp.where(first, pltpu.roll(kt, HEAD_DIM, 1), one_at_head_dim).astype(BF16)
    v_ref[:, :LANES] = kv[:, KV_WIDTH:].astype(BF16)
    v_ref[:, LANES:] = jnp.where(lane == 0, 1.0, 0.0).astype(BF16)

    chunk = 512
    for c in range(2 * D_MODEL // chunk):
        g = proj(_G0 + c * chunk, chunk) + bg_ref[:, c * chunk:(c + 1) * chunk]
        gate_ref[:, c * chunk:(c + 1) * chunk] = jax.nn.sigmoid(g).astype(BF16)

    u_ref[...] = proj(_U0, POOL_WIDTH)


def _in_proj(x2d, g_mix, w_in, b_gate, gq_t, gk_t, qshift, cos_t, shi_t, slo_t, seq):
    rows = x2d.shape[0]
    tm = 1024
    seq_tiles = seq // tm
    const = lambda i: (0, 0)
    row = lambda i: (i, 0)
    pos = lambda i: (i % seq_tiles, 0)
    in_width = w_in.shape[1]
    return pl.pallas_call(
        _in_proj_kernel,
        grid=(rows // tm,),
        in_specs=[
            pl.BlockSpec((tm, D_MODEL), row),
            pl.BlockSpec((1, D_MODEL), const),
            pl.BlockSpec((D_MODEL, in_width), const, pipeline_mode=pl.Buffered(1)),
            pl.BlockSpec((1, 2 * D_MODEL), const),
            pl.BlockSpec((1, LANES), const),
            pl.BlockSpec((1, LANES), const),
            pl.BlockSpec((1, LANES), const),
            pl.BlockSpec((tm, LANES), pos),
            pl.BlockSpec((tm, LANES), pos),
            pl.BlockSpec((tm, LANES), pos),
        ],
        out_specs=[
            pl.BlockSpec((tm, N_Q_HEADS * LANES), row),
            pl.BlockSpec((tm, N_KV_HEADS * LANES), row),
            pl.BlockSpec((tm, 2 * LANES), row),
            pl.BlockSpec((tm, POOL_WIDTH), row),
            pl.BlockSpec((tm, 2 * D_MODEL), row),
        ],
        out_shape=[
            jax.ShapeDtypeStruct((rows, N_Q_HEADS * LANES), BF16),
            jax.ShapeDtypeStruct((rows, N_KV_HEADS * LANES), BF16),
            jax.ShapeDtypeStruct((rows, 2 * LANES), BF16),
            jax.ShapeDtypeStruct((rows, POOL_WIDTH), F32),
            jax.ShapeDtypeStruct((rows, 2 * D_MODEL), BF16),
        ],
        compiler_params=pltpu.CompilerParams(
            dimension_semantics=("arbitrary",), vmem_limit_bytes=VMEM_LIMIT),
        name="in_proj",
    )(x2d, g_mix, w_in, b_gate, gq_t, gk_t, qshift, cos_t, shi_t, slo_t)


def _pooled(i, n_tiles, u_ref, uprev_ref, unext_ref, icnt_ref, slab_ref, wa_ref, wb_ref):
    tq = u_ref.shape[0]
    span = tq + 2 * POOL_HALO
    zero_halo = jnp.zeros((POOL_HALO, POOL_WIDTH), F32)
    slab_ref[0:POOL_HALO, :] = jnp.where(i > 0, uprev_ref[...], zero_halo)
    slab_ref[POOL_HALO:POOL_HALO + tq, :] = u_ref[...]
    slab_ref[POOL_HALO + tq:span, :] = jnp.where(i < n_tiles - 1, unext_ref[...], zero_halo)
    slab_ref[span:, :] = zero_halo
    wa_ref[span:, :] = jnp.zeros((POOL_HALO, 2 * POOL_GROUP), F32)
    wb_ref[span:, :] = jnp.zeros((POOL_HALO, 2 * POOL_GROUP), F32)

    g0 = slice(0, POOL_GROUP)
    g1 = slice(POOL_GROUP, 2 * POOL_GROUP)
    wide = slice(2 * POOL_GROUP, POOL_WIDTH)
    h = POOL_HALO
    win2 = slab_ref[h - 1:h - 1 + tq, g0] + slab_ref[h:h + tq, g0]
    win4 = ((slab_ref[h - 2:h - 2 + tq, g1] + slab_ref[h - 1:h - 1 + tq, g1])
            + (slab_ref[h:h + tq, g1] + slab_ref[h + 1:h + 1 + tq, g1]))
    wa_ref[0:span, :] = slab_ref[0:span, wide] + slab_ref[1:span + 1, wide]
    wb_ref[0:span, :] = wa_ref[0:span, :] + wa_ref[2:span + 2, :]
    wa_ref[0:span, :] = wb_ref[0:span, :] + wb_ref[4:span + 4, :]
    win8 = wa_ref[h - 4:h - 4 + tq, 0:POOL_GROUP]
    win16 = wa_ref[0:tq, POOL_GROUP:] + wa_ref[h:h + tq, POOL_GROUP:]

    pooled = []
    for gi, win in enumerate((win2, win4, win8, win16)):
        cols = slice(gi * POOL_GROUP, (gi + 1) * POOL_GROUP)
        pooled.append((win * icnt_ref[:, cols] - u_ref[:, cols]).astype(BF16))
    return pooled


def _mixer_kernel(q_ref, k_ref, v_ref, u_ref, uprev_ref, unext_ref, icnt_ref, gate_ref, x_ref,
                  wau_ref, pw_ref, ps_ref, wpu_ref, wo_ref, h_ref, slab_ref, wa_ref, wb_ref,
                  attn_ref, *, tq, seq, subtract_max):
    step = pl.program_id(0)
    n_tiles = seq // tq
    i = jnp.maximum(step - 1, 0) % n_tiles

    @pl.when(step == 0)
    def _():
        attn_ref[...] = jnp.zeros_like(attn_ref)

    o_prev = attn_ref[...]

    pooled = _pooled(i, n_tiles, u_ref, uprev_ref, unext_ref, icnt_ref, slab_ref, wa_ref, wb_ref)

    v = v_ref[...]
    lane = lax.broadcasted_iota(jnp.int32, (tq, LANES), 1)
    first = lane < HEAD_DIM
    heads = []
    for h in range(N_Q_HEADS):
        g = h // (N_Q_HEADS // N_KV_HEADS)
        s = lax.dot_general(q_ref[:, h * LANES:(h + 1) * LANES], k_ref[:, g * LANES:(g + 1) * LANES],
                            (((1,), (1,)), ((), ())), preferred_element_type=F32)
        if subtract_max:
            s = s - jnp.max(s, axis=-1, keepdims=True)
        p = jnp.exp2(s).astype(BF16)
        r = jnp.dot(p, v, preferred_element_type=F32)
        heads.append(r[:, :LANES] / r[:, LANES:LANES + 1])
    half = N_Q_HEADS // 2
    attn_ref[...] = jnp.concatenate(
        [jnp.where(first, heads[t], heads[half + t]) for t in range(half)], axis=1).astype(BF16)
    a = jnp.dot(o_prev, wau_ref[...], preferred_element_type=F32)

    pm = [jnp.dot(pooled[gi], pw_ref[gi], preferred_element_type=F32)
          for gi in range(len(POOL_WINDOWS))]
    pm = jnp.concatenate(pm, axis=1) * ps_ref[...]
    p_branch = jnp.dot(pm.astype(BF16), wpu_ref[...], preferred_element_type=F32)

    g_a = gate_ref[:, :D_MODEL].astype(F32)
    g_p = gate_ref[:, D_MODEL:].astype(F32)
    mixed = (g_a * a + g_p * p_branch).astype(BF16)
    h_ref[...] = x_ref[...] + jnp.dot(mixed, wo_ref[...], preferred_element_type=F32)


def _mixer(q, k, v, u, icnt, gates, x2d, wau, pw, ps, wpu, wo, *, batch, seq, subtract_max):
    tq = 512
    single = pl.Buffered(1)
    n_tiles = seq // tq
    total = batch * n_tiles
    halo_blocks = tq // POOL_HALO
    seq_halos = seq // POOL_HALO
    attn_tile = lambda s: jnp.minimum(s, total - 1)
    post_tile = lambda s: jnp.maximum(s - 1, 0)
    attn_row = lambda s: (attn_tile(s), 0)
    per_batch = lambda s: (attn_tile(s) // n_tiles, 0)
    row = lambda s: (post_tile(s), 0)
    seq_row = lambda s: (post_tile(s) % n_tiles, 0)
    const2 = lambda s: (0, 0)
    const3 = lambda s: (0, 0, 0)

    def prev(s):
        t = post_tile(s)
        return ((t // n_tiles) * seq_halos + jnp.maximum((t % n_tiles) * halo_blocks - 1, 0), 0)

    def nxt(s):
        t = post_tile(s)
        return ((t // n_tiles) * seq_halos
                + jnp.minimum((t % n_tiles + 1) * halo_blocks, seq_halos - 1), 0)

    return pl.pallas_call(
        functools.partial(_mixer_kernel, tq=tq, seq=seq, subtract_max=subtract_max),
        grid=(total + 1,),
        in_specs=[
            pl.BlockSpec((tq, N_Q_HEADS * LANES), attn_row),
            pl.BlockSpec((seq, N_KV_HEADS * LANES), per_batch),
            pl.BlockSpec((seq, 2 * LANES), per_batch),
            pl.BlockSpec((tq, POOL_WIDTH), row),
            pl.BlockSpec((POOL_HALO, POOL_WIDTH), prev),
            pl.BlockSpec((POOL_HALO, POOL_WIDTH), nxt),
            pl.BlockSpec((tq, POOL_WIDTH), seq_row),
            pl.BlockSpec((tq, 2 * D_MODEL), row),
            pl.BlockSpec((tq, D_MODEL), row),
            pl.BlockSpec((ATTN_WIDTH, D_MODEL), const2, pipeline_mode=single),
            pl.BlockSpec((len(POOL_WINDOWS), POOL_GROUP, POOL_GROUP), const3, pipeline_mode=single),
            pl.BlockSpec((1, POOL_WIDTH), const2),
            pl.BlockSpec((POOL_WIDTH, D_MODEL), const2, pipeline_mode=single),
            pl.BlockSpec((D_MODEL, D_MODEL), const2, pipeline_mode=single),
        ],
        out_specs=pl.BlockSpec((tq, D_MODEL), row),
        out_shape=jax.ShapeDtypeStruct(x2d.shape, F32),
        scratch_shapes=[pltpu.VMEM((tq + 3 * POOL_HALO, POOL_WIDTH), F32),
                        pltpu.VMEM((tq + 3 * POOL_HALO, 2 * POOL_GROUP), F32),
                        pltpu.VMEM((tq + 3 * POOL_HALO, 2 * POOL_GROUP), F32),
                        pltpu.VMEM((tq, ATTN_WIDTH), BF16)],
        compiler_params=pltpu.CompilerParams(
            dimension_semantics=("arbitrary",), vmem_limit_bytes=VMEM_LIMIT),
        name="mixer_max" if subtract_max else "mixer",
    )(q, k, v, u, u, u, icnt, gates, x2d, wau, pw, ps, wpu, wo)


def _cross_ffn_kernel(h_ref, xk_ref, xv_ref, gc_ref, wq_ref, wo_ref, gf_ref, w1_ref, w2_ref,
                      gfin_ref, out_ref, *, parts):
    rows = h_ref.shape[0] // parts
    sl = [slice(j * rows, (j + 1) * rows) for j in range(parts)]
    every = range(parts)

    h = [h_ref[sl[j], :] for j in every]
    n2 = [(h[j] * _rms_scale(h[j]) * gc_ref[...]).astype(BF16) for j in every]
    xq = [(jnp.dot(n2[j], wq_ref[...], preferred_element_type=F32) * (X_HEAD_DIM ** -0.5))
          .astype(BF16) for j in every]
    outs = [[] for _ in every]
    for hd in range(N_X_HEADS):
        cols = slice(hd * X_HEAD_DIM, (hd + 1) * X_HEAD_DIM)
        for j in every:
            s = lax.dot_general(xq[j][:, cols], xk_ref[:, cols], (((1,), (1,)), ((), ())),
                                preferred_element_type=F32)
            p, l = _softmax_rows(s)
            o = jnp.dot(p.astype(BF16), xv_ref[:, cols], preferred_element_type=F32)
            outs[j].append((o / l).astype(BF16))
    h = [h[j] + jnp.dot(jnp.concatenate(outs[j], axis=1), wo_ref[...], preferred_element_type=F32)
         for j in every]

    n3 = [(h[j] * _rms_scale(h[j]) * gf_ref[...]).astype(BF16) for j in every]
    chunk = 1024
    acc = h
    for c in range(D_FF // chunk):
        t = [jnp.dot(n3[j], w1_ref[:, c * chunk:(c + 1) * chunk], preferred_element_type=F32)
             for j in every]
        t = [jnp.square(jnp.maximum(t[j], 0.0)).astype(BF16) for j in every]
        acc = [acc[j] + jnp.dot(t[j], w2_ref[c * chunk:(c + 1) * chunk, :],
                                preferred_element_type=F32) for j in every]
    for j in every:
        out_ref[sl[j], :] = acc[j] * _rms_scale(acc[j]) * gfin_ref[...]


def _cross_ffn(h2d, xk, xv, g_cross, w_xq, w_xo, g_ffn, w_ff1, w_ff2, g_final, seq, n_mem):
    rows = h2d.shape[0]
    tm = 1024
    tiles_per_batch = seq // tm
    const = lambda i: (0, 0)
    row = lambda i: (i, 0)
    per_batch = lambda i: (i // tiles_per_batch, 0)
    single = pl.Buffered(1)
    return pl.pallas_call(
        functools.partial(_cross_ffn_kernel, parts=4),
        grid=(rows // tm,),
        in_specs=[
            pl.BlockSpec((tm, D_MODEL), row),
            pl.BlockSpec((n_mem, D_MODEL), per_batch),
            pl.BlockSpec((n_mem, D_MODEL), per_batch),
            pl.BlockSpec((1, D_MODEL), const),
            pl.BlockSpec((D_MODEL, D_MODEL), const, pipeline_mode=single),
            pl.BlockSpec((D_MODEL, D_MODEL), const, pipeline_mode=single),
            pl.BlockSpec((1, D_MODEL), const),
            pl.BlockSpec((D_MODEL, D_FF), const, pipeline_mode=single),
            pl.BlockSpec((D_FF, D_MODEL), const, pipeline_mode=single),
            pl.BlockSpec((1, D_MODEL), const),
        ],
        out_specs=pl.BlockSpec((tm, D_MODEL), row),
        out_shape=jax.ShapeDtypeStruct(h2d.shape, F32),
        compiler_params=pltpu.CompilerParams(
            dimension_semantics=("arbitrary",), vmem_limit_bytes=VMEM_LIMIT),
        name="cross_ffn",
    )(h2d, xk, xv, g_cross, w_xq, w_xo, g_ffn, w_ff1, w_ff2, g_final)


def _rope_tables(seq):
    t = lax.broadcasted_iota(jnp.int32, (seq, LANES), 0)
    d = lax.broadcasted_iota(jnp.int32, (seq, LANES), 1) % HEAD_DIM
    freq = d % ROPE_FREQS
    upper_half = (d // ROPE_FREQS) % 2 == 1
    pos = jnp.where(d < HEAD_DIM // 2, t // GRID_W, t % GRID_W).astype(F32)
    inv = ROPE_THETA ** (-(2 * freq).astype(F32) / (2 * ROPE_FREQS))
    ang = pos * inv
    cos, sin = jnp.cos(ang), jnp.sin(ang)
    return cos, jnp.where(upper_half, sin, 0.0), jnp.where(upper_half, 0.0, -sin)


def _pool_inverse_counts(seq):
    t = lax.broadcasted_iota(jnp.int32, (seq, POOL_WIDTH), 0)
    group = lax.broadcasted_iota(jnp.int32, (seq, POOL_WIDTH), 1) // POOL_GROUP
    w = jnp.left_shift(POOL_WINDOWS[0], group)
    assert POOL_WINDOWS == tuple(POOL_WINDOWS[0] << g for g in range(len(POOL_WINDOWS)))
    lo = jnp.maximum(t - w // 2, 0)
    hi = jnp.minimum(t + (w - w // 2), seq)
    return 1.0 / (hi - lo).astype(F32)


def kernel(x, mem, g_mix, w_in, b_gate, g_q, g_k, w_attn_up, pool_w, pool_scale, w_pool_up, w_out,
           g_cross, g_mem, w_xq, w_xkv, w_xo, g_ffn, w_ff1, w_ff2, g_final):
    batch, seq, d = x.shape
    n_mem = mem.shape[1]
    assert d == D_MODEL and w_in.shape[0] == 1 and seq % 1024 == 0 and (batch * n_mem) % 1024 == 0
    l = 0
    x2d = x.reshape(batch * seq, d)
    mem2d = mem.reshape(batch * n_mem, d)
    row = lambda a: a.reshape(1, -1)

    cos_t, shi_t, slo_t = _rope_tables(seq)
    two_heads = lambda g: jnp.concatenate([g, g]).reshape(1, LANES)
    gq_t = two_heads(g_q[l] * (HEAD_DIM ** -0.5 * LOG2_E))
    gk_t = two_heads(g_k[l])
    shift = 1.02 * HEAD_DIM ** 0.5 * jnp.max(jnp.abs(g_q[l])) * jnp.max(jnp.abs(g_k[l])) * LOG2_E
    fixed_shift_ok = shift <= MAX_FIXED_SHIFT * LOG2_E
    qshift = jnp.where((jnp.arange(LANES) == HEAD_DIM) & fixed_shift_ok, -shift, 0.0)
    qshift = qshift.astype(F32).reshape(1, LANES)
    half = N_Q_HEADS // 2
    wau = (w_attn_up[l].reshape(2, half, HEAD_DIM, d).transpose(1, 0, 2, 3)
           .reshape(ATTN_WIDTH, d).astype(BF16))

    xk, xv = _mem_kv(mem2d, row(g_mem[l]), w_xkv[l].astype(BF16))
    q, k, v, u, gates = _in_proj(x2d, row(g_mix[l]), w_in[l].astype(BF16), row(b_gate[l]),
                                 gq_t, gk_t, qshift, cos_t, shi_t, slo_t, seq)
    mixer_args = (q, k, v, u, _pool_inverse_counts(seq), gates, x2d, wau, pool_w[l].astype(BF16), row(pool_scale[l]),
                  w_pool_up[l].astype(BF16), w_out[l].astype(BF16))
    h1 = lax.cond(
        fixed_shift_ok,
        lambda *a: _mixer(*a, batch=batch, seq=seq, subtract_max=False),
        lambda *a: _mixer(*a, batch=batch, seq=seq, subtract_max=True),
        *mixer_args)
    out = _cross_ffn(h1, xk, xv, row(g_cross[l]), w_xq[l].astype(BF16), w_xo[l].astype(BF16),
                     row(g_ffn[l]), w_ff1[l].astype(BF16), w_ff2[l].astype(BF16), row(g_final),
                     seq, n_mem)
    return out.reshape(batch, seq, d)
```

```python
import functools
import math

import jax
import jax.numpy as jnp
from jax import lax
from jax.experimental import pallas as pl
from jax.experimental.pallas import tpu as pltpu

D_MODEL = 1024
GRID_W = 64
HEAD_DIM = 64
N_Q_HEADS = 8
N_KV_HEADS = 2
ATTN_WIDTH = N_Q_HEADS * HEAD_DIM
KV_WIDTH = N_KV_HEADS * HEAD_DIM
POOL_WINDOWS = (2, 4, 8, 16)
POOL_WIDTH = 512
POOL_GROUP = 128
ROPE_THETA = 10000.0
ROPE_FREQS = 16
N_X_HEADS = 4
X_HEAD_DIM = 256
D_FF = 4096
EPS = 1e-6

LANES = 128
POOL_HALO = 8
VMEM_LIMIT = 56 * 1024 * 1024
LOG2_E = math.log2(math.e)
MAX_FIXED_SHIFT = 40.0

BF16 = jnp.bfloat16
F32 = jnp.float32

_Q0, _K0, _V0, _U0, _G0 = 0, 512, 640, 768, 1280


def _rms_scale(x):
    return lax.rsqrt(jnp.mean(x * x, axis=-1, keepdims=True) + EPS)


def _head_rms_scale(x128):
    lane = lax.broadcasted_iota(jnp.int32, x128.shape, 1)
    first = lane < HEAD_DIM
    sq = x128 * x128
    s_a = jnp.sum(jnp.where(first, sq, 0.0), axis=-1, keepdims=True)
    s_b = jnp.sum(jnp.where(first, 0.0, sq), axis=-1, keepdims=True)
    r_a = lax.rsqrt(s_a * (1.0 / HEAD_DIM) + EPS)
    r_b = lax.rsqrt(s_b * (1.0 / HEAD_DIM) + EPS)
    return jnp.where(first, r_a, r_b)


def _rope(xn, cos, sin_hi, sin_lo):
    return (xn * cos
            + pltpu.roll(xn, ROPE_FREQS, 1) * sin_hi
            + pltpu.roll(xn, LANES - ROPE_FREQS, 1) * sin_lo)


def _softmax_rows(s):
    m = jnp.max(s, axis=-1, keepdims=True)
    p = jnp.exp(s - m)
    return p, jnp.sum(p, axis=-1, keepdims=True)


def _cast_specs(weights, n_blocks, step_to_block=lambda s: s):
    in_specs, out_specs, out_shapes = [], [], []
    for w in weights:
        rows, cols = w.shape
        assert rows % (n_blocks * 16) == 0, (w.shape, n_blocks)
        index_map = lambda s: (step_to_block(s), 0)
        in_specs.append(pl.BlockSpec((rows // n_blocks, cols), index_map))
        out_specs.append(pl.BlockSpec((rows // n_blocks, cols), index_map))
        out_shapes.append(jax.ShapeDtypeStruct(w.shape, BF16))
    return in_specs, out_specs, out_shapes


def _cast_blocks(src_refs, dst_refs):
    assert len(src_refs) == len(dst_refs)
    for src, dst in zip(src_refs, dst_refs):
        dst[...] = src[...].astype(BF16)


def _mem_kv_kernel(mem_ref, g_ref, w_ref, cast_src, xk_ref, xv_ref, cast_dst):
    _cast_blocks([cast_src], [cast_dst])
    m = mem_ref[...]
    mn = (m * _rms_scale(m) * g_ref[...]).astype(BF16)
    kv = jnp.dot(mn, w_ref[...], preferred_element_type=F32)
    xk_ref[...] = kv[:, :D_MODEL].astype(BF16)
    xv_ref[...] = kv[:, D_MODEL:].astype(BF16)


def _mem_kv(mem2d, g_mem, w_xkv, w_to_cast):
    rows = mem2d.shape[0]
    tm = 1024
    steps = rows // tm
    const = lambda i: (0, 0)
    cast_in, cast_out, cast_shapes = _cast_specs([w_to_cast], steps)
    return pl.pallas_call(
        _mem_kv_kernel,
        grid=(steps,),
        in_specs=[
            pl.BlockSpec((tm, D_MODEL), lambda i: (i, 0)),
            pl.BlockSpec((1, D_MODEL), const),
            pl.BlockSpec((D_MODEL, 2 * D_MODEL), const),
        ] + cast_in,
        out_specs=[
            pl.BlockSpec((tm, D_MODEL), lambda i: (i, 0)),
            pl.BlockSpec((tm, D_MODEL), lambda i: (i, 0)),
        ] + cast_out,
        out_shape=[jax.ShapeDtypeStruct((rows, D_MODEL), BF16)] * 2 + cast_shapes,
        compiler_params=pltpu.CompilerParams(
            dimension_semantics=("arbitrary",), vmem_limit_bytes=VMEM_LIMIT),
        name="mem_kv",
    )(mem2d, g_mem, w_xkv, w_to_cast)


def _in_proj_kernel(x_ref, gmix_ref, w_ref, bg_ref, gq_ref, gk_ref, qshift_ref,
                    cos_ref, shi_ref, slo_ref, *rest, n_cast):
    cast_src, rest = rest[:n_cast], rest[n_cast:]
    (q_ref, k_ref, v_ref, u_ref, gate_ref), cast_dst = rest[:5], rest[5:]
    _cast_blocks(cast_src, cast_dst)
    x = x_ref[...]
    n1 = (x * _rms_scale(x) * gmix_ref[...]).astype(BF16)
    cos, shi, slo = cos_ref[...], shi_ref[...], slo_ref[...]
    lane = lax.broadcasted_iota(jnp.int32, (x.shape[0], LANES), 1)
    first = lane < HEAD_DIM

    def proj(c0, width):
        return jnp.dot(n1, w_ref[:, c0:c0 + width], preferred_element_type=F32)

    q = proj(_Q0, ATTN_WIDTH)
    gq = gq_ref[...]
    qshift = qshift_ref[...]
    for t in range(ATTN_WIDTH // LANES):
        qt = q[:, t * LANES:(t + 1) * LANES]
        qt = _rope(qt * _head_rms_scale(qt) * gq, cos, shi, slo)
        q_ref[:, (2 * t) * LANES:(2 * t + 1) * LANES] = jnp.where(first, qt, qshift).astype(BF16)
        q_ref[:, (2 * t + 1) * LANES:(2 * t + 2) * LANES] = jnp.where(
            first, pltpu.roll(qt, HEAD_DIM, 1), qshift).astype(BF16)

    kv = proj(_K0, 2 * KV_WIDTH)
    kt = kv[:, :KV_WIDTH]
    kt = _rope(kt * _head_rms_scale(kt) * gk_ref[...], cos, shi, slo)
    one_at_head_dim = jnp.where(lane == HEAD_DIM, 1.0, 0.0)
    k_ref[:, :LANES] = jnp.where(first, kt, one_at_head_dim).astype(BF16)
    k_ref[:, LANES:] = jnp.where(first, pltpu.roll(kt, HEAD_DIM, 1), one_at_head_dim).astype(BF16)
    v_ref[:, :LANES] = kv[:, KV_WIDTH:].astype(BF16)
    v_ref[:, LANES:] = jnp.where(lane == 0, 1.0, 0.0).astype(BF16)

    chunk = 512
    for c in range(2 * D_MODEL // chunk):
        g = proj(_G0 + c * chunk, chunk) + bg_ref[:, c * chunk:(c + 1) * chunk]
        gate_ref[:, c * chunk:(c + 1) * chunk] = jax.nn.sigmoid(g).astype(BF16)

    u_ref[...] = proj(_U0, POOL_WIDTH)


def _in_proj(x2d, g_mix, w_in, b_gate, gq_t, gk_t, qshift, cos_t, shi_t, slo_t, seq, to_cast):
    rows = x2d.shape[0]
    tm = 1024
    cast_in, cast_out, cast_shapes = _cast_specs(to_cast, rows // tm)
    seq_tiles = seq // tm
    const = lambda i: (0, 0)
    row = lambda i: (i, 0)
    pos = lambda i: (i % seq_tiles, 0)
    in_width = w_in.shape[1]
    return pl.pallas_call(
        functools.partial(_in_proj_kernel, n_cast=len(to_cast)),
        grid=(rows // tm,),
        in_specs=[
            pl.BlockSpec((tm, D_MODEL), row),
            pl.BlockSpec((1, D_MODEL), const),
            pl.BlockSpec((D_MODEL, in_width), const, pipeline_mode=pl.Buffered(1)),
            pl.BlockSpec((1, 2 * D_MODEL), const),
            pl.BlockSpec((1, LANES), const),
            pl.BlockSpec((1, LANES), const),
            pl.BlockSpec((1, LANES), const),
            pl.BlockSpec((tm, LANES), pos),
            pl.BlockSpec((tm, LANES), pos),
            pl.BlockSpec((tm, LANES), pos),
        ] + cast_in,
        out_specs=[
            pl.BlockSpec((tm, N_Q_HEADS * LANES), row),
            pl.BlockSpec((tm, N_KV_HEADS * LANES), row),
            pl.BlockSpec((tm, 2 * LANES), row),
            pl.BlockSpec((tm, POOL_WIDTH), row),
            pl.BlockSpec((tm, 2 * D_MODEL), row),
        ] + cast_out,
        out_shape=[
            jax.ShapeDtypeStruct((rows, N_Q_HEADS * LANES), BF16),
            jax.ShapeDtypeStruct((rows, N_KV_HEADS * LANES), BF16),
            jax.ShapeDtypeStruct((rows, 2 * LANES), BF16),
            jax.ShapeDtypeStruct((rows, POOL_WIDTH), F32),
            jax.ShapeDtypeStruct((rows, 2 * D_MODEL), BF16),
        ] + cast_shapes,
        compiler_params=pltpu.CompilerParams(
            dimension_semantics=("arbitrary",), vmem_limit_bytes=VMEM_LIMIT),
        name="in_proj",
    )(x2d, g_mix, w_in, b_gate, gq_t, gk_t, qshift, cos_t, shi_t, slo_t, *to_cast)


def _pooled(i, n_tiles, u_ref, uprev_ref, unext_ref, icnt_ref, slab_ref, wa_ref, wb_ref):
    tq = u_ref.shape[0]
    span = tq + 2 * POOL_HALO
    zero_halo = jnp.zeros((POOL_HALO, POOL_WIDTH), F32)
    slab_ref[0:POOL_HALO, :] = jnp.where(i > 0, uprev_ref[...], zero_halo)
    slab_ref[POOL_HALO:POOL_HALO + tq, :] = u_ref[...]
    slab_ref[POOL_HALO + tq:span, :] = jnp.where(i < n_tiles - 1, unext_ref[...], zero_halo)
    slab_ref[span:, :] = zero_halo
    wa_ref[span:, :] = jnp.zeros((POOL_HALO, 2 * POOL_GROUP), F32)
    wb_ref[span:, :] = jnp.zeros((POOL_HALO, 2 * POOL_GROUP), F32)

    g0 = slice(0, POOL_GROUP)
    g1 = slice(POOL_GROUP, 2 * POOL_GROUP)
    wide = slice(2 * POOL_GROUP, POOL_WIDTH)
    h = POOL_HALO
    win2 = slab_ref[h - 1:h - 1 + tq, g0] + slab_ref[h:h + tq, g0]
    win4 = ((slab_ref[h - 2:h - 2 + tq, g1] + slab_ref[h - 1:h - 1 + tq, g1])
            + (slab_ref[h:h + tq, g1] + slab_ref[h + 1:h + 1 + tq, g1]))
    wa_ref[0:span, :] = slab_ref[0:span, wide] + slab_ref[1:span + 1, wide]
    wb_ref[0:span, :] = wa_ref[0:span, :] + wa_ref[2:span + 2, :]
    wa_ref[0:span, :] = wb_ref[0:span, :] + wb_ref[4:span + 4, :]
    win8 = wa_ref[h - 4:h - 4 + tq, 0:POOL_GROUP]
    win16 = wa_ref[0:tq, POOL_GROUP:] + wa_ref[h:h + tq, POOL_GROUP:]

    pooled = []
    for gi, win in enumerate((win2, win4, win8, win16)):
        cols = slice(gi * POOL_GROUP, (gi + 1) * POOL_GROUP)
        pooled.append((win * icnt_ref[:, cols] - u_ref[:, cols]).astype(BF16))
    return pooled


def _mixer_kernel(q_ref, k_ref, v_ref, u_ref, uprev_ref, unext_ref, icnt_ref, gate_ref, x_ref,
                  wau_ref, pw_ref, ps_ref, wpu_ref, wo_ref, *rest, tq, seq, subtract_max, n_cast):
    cast_src, h_ref, rest = rest[:n_cast], rest[n_cast], rest[n_cast + 1:]
    cast_dst, (slab_ref, wa_ref, wb_ref, attn_ref) = rest[:n_cast], rest[n_cast:]
    _cast_blocks(cast_src, cast_dst)
    step = pl.program_id(0)
    n_tiles = seq // tq
    i = jnp.maximum(step - 1, 0) % n_tiles

    @pl.when(step == 0)
    def _():
        attn_ref[...] = jnp.zeros_like(attn_ref)

    o_prev = attn_ref[...]

    pooled = _pooled(i, n_tiles, u_ref, uprev_ref, unext_ref, icnt_ref, slab_ref, wa_ref, wb_ref)

    v = v_ref[...]
    lane = lax.broadcasted_iota(jnp.int32, (tq, LANES), 1)
    first = lane < HEAD_DIM
    heads = []
    for h in range(N_Q_HEADS):
        g = h // (N_Q_HEADS // N_KV_HEADS)
        s = lax.dot_general(q_ref[:, h * LANES:(h + 1) * LANES], k_ref[:, g * LANES:(g + 1) * LANES],
                            (((1,), (1,)), ((), ())), preferred_element_type=F32)
        if subtract_max:
            s = s - jnp.max(s, axis=-1, keepdims=True)
        p = jnp.exp2(s).astype(BF16)
        r = jnp.dot(p, v, preferred_element_type=F32)
        heads.append(r[:, :LANES] / r[:, LANES:LANES + 1])
    half = N_Q_HEADS // 2
    attn_ref[...] = jnp.concatenate(
        [jnp.where(first, heads[t], heads[half + t]) for t in range(half)], axis=1).astype(BF16)
    a = jnp.dot(o_prev, wau_ref[...], preferred_element_type=F32)

    pm = [jnp.dot(pooled[gi], pw_ref[gi], preferred_element_type=F32)
          for gi in range(len(POOL_WINDOWS))]
    pm = jnp.concatenate(pm, axis=1) * ps_ref[...]
    p_branch = jnp.dot(pm.astype(BF16), wpu_ref[...], preferred_element_type=F32)

    g_a = gate_ref[:, :D_MODEL].astype(F32)
    g_p = gate_ref[:, D_MODEL:].astype(F32)
    mixed = (g_a * a + g_p * p_branch).astype(BF16)
    h_ref[...] = x_ref[...] + jnp.dot(mixed, wo_ref[...], preferred_element_type=F32)


def _mixer(q, k, v, u, icnt, gates, x2d, wau, pw, ps, wpu, wo, *to_cast, batch, seq,
           subtract_max):
    tq = 512
    n_tiles = seq // tq
    total = batch * n_tiles
    cast_in, cast_out, cast_shapes = _cast_specs(to_cast, total, lambda s: jnp.minimum(s, total - 1))
    halo_blocks = tq // POOL_HALO
    seq_halos = seq // POOL_HALO
    attn_tile = lambda s: jnp.minimum(s, total - 1)
    post_tile = lambda s: jnp.maximum(s - 1, 0)
    attn_row = lambda s: (attn_tile(s), 0)
    per_batch = lambda s: (attn_tile(s) // n_tiles, 0)
    row = lambda s: (post_tile(s), 0)
    seq_row = lambda s: (post_tile(s) % n_tiles, 0)
    const2 = lambda s: (0, 0)
    const3 = lambda s: (0, 0, 0)

    def prev(s):
        t = post_tile(s)
        return ((t // n_tiles) * seq_halos + jnp.maximum((t % n_tiles) * halo_blocks - 1, 0), 0)

    def nxt(s):
        t = post_tile(s)
        return ((t // n_tiles) * seq_halos
                + jnp.minimum((t % n_tiles + 1) * halo_blocks, seq_halos - 1), 0)

    return pl.pallas_call(
        functools.partial(_mixer_kernel, tq=tq, seq=seq, subtract_max=subtract_max,
                          n_cast=len(to_cast)),
        grid=(total + 1,),
        in_specs=[
            pl.BlockSpec((tq, N_Q_HEADS * LANES), attn_row),
            pl.BlockSpec((seq, N_KV_HEADS * LANES), per_batch),
            pl.BlockSpec((seq, 2 * LANES), per_batch),
            pl.BlockSpec((tq, POOL_WIDTH), row),
            pl.BlockSpec((POOL_HALO, POOL_WIDTH), prev),
            pl.BlockSpec((POOL_HALO, POOL_WIDTH), nxt),
            pl.BlockSpec((tq, POOL_WIDTH), seq_row),
            pl.BlockSpec((tq, 2 * D_MODEL), row),
            pl.BlockSpec((tq, D_MODEL), row),
            pl.BlockSpec((ATTN_WIDTH, D_MODEL), const2),
            pl.BlockSpec((len(POOL_WINDOWS), POOL_GROUP, POOL_GROUP), const3),
            pl.BlockSpec((1, POOL_WIDTH), const2),
            pl.BlockSpec((POOL_WIDTH, D_MODEL), const2),
            pl.BlockSpec((D_MODEL, D_MODEL), const2),
        ] + cast_in,
        out_specs=[pl.BlockSpec((tq, D_MODEL), row)] + cast_out,
        out_shape=[jax.ShapeDtypeStruct(x2d.shape, F32)] + cast_shapes,
        scratch_shapes=[pltpu.VMEM((tq + 3 * POOL_HALO, POOL_WIDTH), F32),
                        pltpu.VMEM((tq + 3 * POOL_HALO, 2 * POOL_GROUP), F32),
                        pltpu.VMEM((tq + 3 * POOL_HALO, 2 * POOL_GROUP), F32),
                        pltpu.VMEM((tq, ATTN_WIDTH), BF16)],
        compiler_params=pltpu.CompilerParams(
            dimension_semantics=("arbitrary",), vmem_limit_bytes=VMEM_LIMIT),
        name="mixer_max" if subtract_max else "mixer",
    )(q, k, v, u, u, u, icnt, gates, x2d, wau, pw, ps, wpu, wo, *to_cast)


def _cross_ffn_kernel(h_ref, xk_ref, xv_ref, gc_ref, wq_ref, wo_ref, gf_ref, w1_ref, w2_ref,
                      gfin_ref, out_ref, *, parts):
    rows = h_ref.shape[0] // parts
    sl = [slice(j * rows, (j + 1) * rows) for j in range(parts)]
    every = range(parts)

    h = [h_ref[sl[j], :] for j in every]
    n2 = [(h[j] * _rms_scale(h[j]) * gc_ref[...]).astype(BF16) for j in every]
    xq = [(jnp.dot(n2[j], wq_ref[...], preferred_element_type=F32) * (X_HEAD_DIM ** -0.5))
          .astype(BF16) for j in every]
    outs = [[] for _ in every]
    for hd in range(N_X_HEADS):
        cols = slice(hd * X_HEAD_DIM, (hd + 1) * X_HEAD_DIM)
        for j in every:
            s = lax.dot_general(xq[j][:, cols], xk_ref[:, cols], (((1,), (1,)), ((), ())),
                                preferred_element_type=F32)
            p, l = _softmax_rows(s)
            o = jnp.dot(p.astype(BF16), xv_ref[:, cols], preferred_element_type=F32)
            outs[j].append((o / l).astype(BF16))
    h = [h[j] + jnp.dot(jnp.concatenate(outs[j], axis=1), wo_ref[...], preferred_element_type=F32)
         for j in every]

    n3 = [(h[j] * _rms_scale(h[j]) * gf_ref[...]).astype(BF16) for j in every]
    chunk = 1024
    acc = h
    for c in range(D_FF // chunk):
        t = [jnp.dot(n3[j], w1_ref[:, c * chunk:(c + 1) * chunk], preferred_element_type=F32)
             for j in every]
        t = [jnp.square(jnp.maximum(t[j], 0.0)).astype(BF16) for j in every]
        acc = [acc[j] + jnp.dot(t[j], w2_ref[c * chunk:(c + 1) * chunk, :],
                                preferred_element_type=F32) for j in every]
    for j in every:
        out_ref[sl[j], :] = acc[j] * _rms_scale(acc[j]) * gfin_ref[...]


def _cross_ffn(h2d, xk, xv, g_cross, w_xq, w_xo, g_ffn, w_ff1, w_ff2, g_final, seq, n_mem):
    rows = h2d.shape[0]
    tm = 1024
    tiles_per_batch = seq // tm
    const = lambda i: (0, 0)
    row = lambda i: (i, 0)
    per_batch = lambda i: (i // tiles_per_batch, 0)
    single = pl.Buffered(1)
    return pl.pallas_call(
        functools.partial(_cross_ffn_kernel, parts=4),
        grid=(rows // tm,),
        in_specs=[
            pl.BlockSpec((tm, D_MODEL), row),
            pl.BlockSpec((n_mem, D_MODEL), per_batch),
            pl.BlockSpec((n_mem, D_MODEL), per_batch),
            pl.BlockSpec((1, D_MODEL), const),
            pl.BlockSpec((D_MODEL, D_MODEL), const, pipeline_mode=single),
            pl.BlockSpec((D_MODEL, D_MODEL), const, pipeline_mode=single),
            pl.BlockSpec((1, D_MODEL), const),
            pl.BlockSpec((D_MODEL, D_FF), const, pipeline_mode=single),
            pl.BlockSpec((D_FF, D_MODEL), const, pipeline_mode=single),
            pl.BlockSpec((1, D_MODEL), const),
        ],
        out_specs=pl.BlockSpec((tm, D_MODEL), row),
        out_shape=jax.ShapeDtypeStruct(h2d.shape, F32),
        compiler_params=pltpu.CompilerParams(
            dimension_semantics=("arbitrary",), vmem_limit_bytes=VMEM_LIMIT),
        name="cross_ffn",
    )(h2d, xk, xv, g_cross, w_xq, w_xo, g_ffn, w_ff1, w_ff2, g_final)


def _rope_tables(seq):
    t = lax.broadcasted_iota(jnp.int32, (seq, LANES), 0)
    d = lax.broadcasted_iota(jnp.int32, (seq, LANES), 1) % HEAD_DIM
    freq = d % ROPE_FREQS
    upper_half = (d // ROPE_FREQS) % 2 == 1
    pos = jnp.where(d < HEAD_DIM // 2, t // GRID_W, t % GRID_W).astype(F32)
    inv = ROPE_THETA ** (-(2 * freq).astype(F32) / (2 * ROPE_FREQS))
    ang = pos * inv
    cos, sin = jnp.cos(ang), jnp.sin(ang)
    return cos, jnp.where(upper_half, sin, 0.0), jnp.where(upper_half, 0.0, -sin)


def _pool_inverse_counts(seq):
    t = lax.broadcasted_iota(jnp.int32, (seq, POOL_WIDTH), 0)
    group = lax.broadcasted_iota(jnp.int32, (seq, POOL_WIDTH), 1) // POOL_GROUP
    w = jnp.left_shift(POOL_WINDOWS[0], group)
    assert POOL_WINDOWS == tuple(POOL_WINDOWS[0] << g for g in range(len(POOL_WINDOWS)))
    lo = jnp.maximum(t - w // 2, 0)
    hi = jnp.minimum(t + (w - w // 2), seq)
    return 1.0 / (hi - lo).astype(F32)


def kernel(x, mem, g_mix, w_in, b_gate, g_q, g_k, w_attn_up, pool_w, pool_scale, w_pool_up, w_out,
           g_cross, g_mem, w_xq, w_xkv, w_xo, g_ffn, w_ff1, w_ff2, g_final):
    batch, seq, d = x.shape
    n_mem = mem.shape[1]
    assert d == D_MODEL and w_in.shape[0] == 1 and seq % 1024 == 0 and (batch * n_mem) % 1024 == 0
    l = 0
    x2d = x.reshape(batch * seq, d)
    mem2d = mem.reshape(batch * n_mem, d)
    row = lambda a: a.reshape(1, -1)

    cos_t, shi_t, slo_t = _rope_tables(seq)
    two_heads = lambda g: jnp.concatenate([g, g]).reshape(1, LANES)
    gq_t = two_heads(g_q[l] * (HEAD_DIM ** -0.5 * LOG2_E))
    gk_t = two_heads(g_k[l])
    shift = 1.02 * HEAD_DIM ** 0.5 * jnp.max(jnp.abs(g_q[l])) * jnp.max(jnp.abs(g_k[l])) * LOG2_E
    fixed_shift_ok = shift <= MAX_FIXED_SHIFT * LOG2_E
    qshift = jnp.where((jnp.arange(LANES) == HEAD_DIM) & fixed_shift_ok, -shift, 0.0)
    qshift = qshift.astype(F32).reshape(1, LANES)
    half = N_Q_HEADS // 2
    wau = w_attn_up[l].reshape(2, half, HEAD_DIM, d).transpose(1, 0, 2, 3).reshape(ATTN_WIDTH, d)
    n_groups = len(POOL_WINDOWS)

    xk, xv, w_in_b = _mem_kv(mem2d, row(g_mem[l]), w_xkv[l].astype(BF16), w_in[l])
    q, k, v, u, gates, wau_b, pw_b, wpu_b, wo_b = _in_proj(
        x2d, row(g_mix[l]), w_in_b, row(b_gate[l]), gq_t, gk_t, qshift, cos_t, shi_t, slo_t, seq,
        [wau, pool_w[l].reshape(n_groups * POOL_GROUP, POOL_GROUP), w_pool_up[l], w_out[l]])
    mixer_args = (q, k, v, u, _pool_inverse_counts(seq), gates, x2d, wau_b,
                  pw_b.reshape(n_groups, POOL_GROUP, POOL_GROUP), row(pool_scale[l]), wpu_b, wo_b,
                  w_xq[l], w_xo[l], w_ff1[l], w_ff2[l])
    h1, w_xq_b, w_xo_b, w_ff1_b, w_ff2_b = lax.cond(
        fixed_shift_ok,
        lambda *a: _mixer(*a, batch=batch, seq=seq, subtract_max=False),
        lambda *a: _mixer(*a, batch=batch, seq=seq, subtract_max=True),
        *mixer_args)
    out = _cross_ffn(h1, xk, xv, row(g_cross[l]), w_xq_b, w_xo_b, row(g_ffn[l]), w_ff1_b, w_ff2_b,
                     row(g_final), seq, n_mem)
    return out.reshape(batch, seq, d)
```

```python
import functools
import math

import jax
import jax.numpy as jnp
import numpy as np
from jax import lax
from jax.experimental import pallas as pl
from jax.experimental.pallas import tpu as pltpu

D_MODEL = 1024
GRID_W = 64
HEAD_DIM = 64
N_Q_HEADS = 8
N_KV_HEADS = 2
ATTN_WIDTH = N_Q_HEADS * HEAD_DIM
KV_WIDTH = N_KV_HEADS * HEAD_DIM
POOL_WINDOWS = (2, 4, 8, 16)
POOL_WIDTH = 512
POOL_GROUP = 128
ROPE_THETA = 10000.0
ROPE_FREQS = 16
N_X_HEADS = 4
X_HEAD_DIM = 256
D_FF = 4096
EPS = 1e-6

LANES = 128
POOL_HALO = 8
VMEM_LIMIT = 56 * 1024 * 1024
LOG2_E = math.log2(math.e)
MAX_FIXED_SHIFT = 40.0

BF16 = jnp.bfloat16
F32 = jnp.float32

_Q0, _K0, _V0, _U0, _G0 = 0, 512, 640, 768, 1280


def _rms_scale(x):
    return lax.rsqrt(jnp.mean(x * x, axis=-1, keepdims=True) + EPS)


def _head_rms_scale(x128):
    lane = lax.broadcasted_iota(jnp.int32, x128.shape, 1)
    first = lane < HEAD_DIM
    sq = x128 * x128
    s_a = jnp.sum(jnp.where(first, sq, 0.0), axis=-1, keepdims=True)
    s_b = jnp.sum(jnp.where(first, 0.0, sq), axis=-1, keepdims=True)
    r_a = lax.rsqrt(s_a * (1.0 / HEAD_DIM) + EPS)
    r_b = lax.rsqrt(s_b * (1.0 / HEAD_DIM) + EPS)
    return jnp.where(first, r_a, r_b)


def _rope(xn, cos, sin_hi, sin_lo):
    return (xn * cos
            + pltpu.roll(xn, ROPE_FREQS, 1) * sin_hi
            + pltpu.roll(xn, LANES - ROPE_FREQS, 1) * sin_lo)


def _softmax_rows(s):
    m = jnp.max(s, axis=-1, keepdims=True)
    p = jnp.exp(s - m)
    return p, jnp.sum(p, axis=-1, keepdims=True)


def _cast_specs(weights, n_blocks, step_to_block=lambda s: s, source_blocks=None):
    in_specs, out_specs, out_shapes = [], [], []
    for i, w in enumerate(weights):
        rows, cols = w.shape
        assert rows % (n_blocks * 16) == 0, (w.shape, n_blocks)
        source = (source_blocks or {}).get(i, lambda b: b)
        in_specs.append(pl.BlockSpec((rows // n_blocks, cols),
                                     lambda s, source=source: (source(step_to_block(s)), 0)))
        out_specs.append(pl.BlockSpec((rows // n_blocks, cols), lambda s: (step_to_block(s), 0)))
        out_shapes.append(jax.ShapeDtypeStruct(w.shape, BF16))
    return in_specs, out_specs, out_shapes


def _cast_blocks(src_refs, dst_refs):
    assert len(src_refs) == len(dst_refs)
    for src, dst in zip(src_refs, dst_refs):
        dst[...] = src[...].astype(BF16)


def _fold_pool_block(pw_ref, ps_ref, wpu_ref, wf_ref):
    scaled = pw_ref[...] * ps_ref[0]
    wf_ref[...] = jnp.dot(scaled, wpu_ref[...], preferred_element_type=F32,
                          precision=lax.Precision.HIGHEST).astype(BF16)


def _mem_kv_kernel(mem_ref, g_ref, w_ref, cast_src, xk_ref, xv_ref, cast_dst):
    _cast_blocks([cast_src], [cast_dst])
    m = mem_ref[...]
    mn = (m * _rms_scale(m) * g_ref[...]).astype(BF16)
    kv = jnp.dot(mn, w_ref[...], preferred_element_type=F32)
    xk_ref[...] = kv[:, :D_MODEL].astype(BF16)
    xv_ref[...] = kv[:, D_MODEL:].astype(BF16)


def _mem_kv(mem2d, g_mem, w_xkv, w_to_cast):
    rows = mem2d.shape[0]
    tm = 1024
    steps = rows // tm
    const = lambda i: (0, 0)
    cast_in, cast_out, cast_shapes = _cast_specs([w_to_cast], steps)
    return pl.pallas_call(
        _mem_kv_kernel,
        grid=(steps,),
        in_specs=[
            pl.BlockSpec((tm, D_MODEL), lambda i: (i, 0)),
            pl.BlockSpec((1, D_MODEL), const),
            pl.BlockSpec((D_MODEL, 2 * D_MODEL), const),
        ] + cast_in,
        out_specs=[
            pl.BlockSpec((tm, D_MODEL), lambda i: (i, 0)),
            pl.BlockSpec((tm, D_MODEL), lambda i: (i, 0)),
        ] + cast_out,
        out_shape=[jax.ShapeDtypeStruct((rows, D_MODEL), BF16)] * 2 + cast_shapes,
        compiler_params=pltpu.CompilerParams(
            dimension_semantics=("arbitrary",), vmem_limit_bytes=VMEM_LIMIT),
        name="mem_kv",
    )(mem2d, g_mem, w_xkv, w_to_cast)


def _in_proj_kernel(x_ref, gmix_ref, w_ref, bg_ref, gq_ref, gk_ref, qshift_ref,
                    cos_ref, shi_ref, slo_ref, *rest, n_cast):
    cast_src, (pw_ref, ps_ref, wpu_ref), rest = rest[:n_cast], rest[n_cast:n_cast + 3], rest[n_cast + 3:]
    (q_ref, k_ref, v_ref, u_ref, gate_ref), cast_dst, wf_ref = rest[:5], rest[5:5 + n_cast], rest[5 + n_cast]
    _cast_blocks(cast_src, cast_dst)
    _fold_pool_block(pw_ref, ps_ref, wpu_ref, wf_ref)
    x = x_ref[...]
    n1 = (x * _rms_scale(x) * gmix_ref[...]).astype(BF16)
    cos, shi, slo = cos_ref[...], shi_ref[...], slo_ref[...]
    lane = lax.broadcasted_iota(jnp.int32, (x.shape[0], LANES), 1)
    first = lane < HEAD_DIM

    def proj(c0, width):
        return jnp.dot(n1, w_ref[:, c0:c0 + width], preferred_element_type=F32)

    q = proj(_Q0, ATTN_WIDTH)
    gq = gq_ref[...]
    qshift = qshift_ref[...]
    for t in range(ATTN_WIDTH // LANES):
        qt = q[:, t * LANES:(t + 1) * LANES]
        qt = _rope(qt * _head_rms_scale(qt) * gq, cos, shi, slo)
        q_ref[:, (2 * t) * LANES:(2 * t + 1) * LANES] = jnp.where(first, qt, qshift).astype(BF16)
        q_ref[:, (2 * t + 1) * LANES:(2 * t + 2) * LANES] = jnp.where(
            first, pltpu.roll(qt, HEAD_DIM, 1), qshift).astype(BF16)

    kv = proj(_K0, 2 * KV_WIDTH)
    kt = kv[:, :KV_WIDTH]
    kt = _rope(kt * _head_rms_scale(kt) * gk_ref[...], cos, shi, slo)
    one_at_head_dim = jnp.where(lane == HEAD_DIM, 1.0, 0.0)
    k_ref[:, :LANES] = jnp.where(first, kt, one_at_head_dim).astype(BF16)
    k_ref[:, LANES:] = jnp.where(first, pltpu.roll(kt, HEAD_DIM, 1), one_at_head_dim).astype(BF16)
    v_ref[:, :LANES] = kv[:, KV_WIDTH:].astype(BF16)
    v_ref[:, LANES:] = jnp.where(lane == 0, 1.0, 0.0).astype(BF16)

    chunk = 512
    for c in range(2 * D_MODEL // chunk):
        g = proj(_G0 + c * chunk, chunk) + bg_ref[:, c * chunk:(c + 1) * chunk]
        gate_ref[:, c * chunk:(c + 1) * chunk] = jax.nn.sigmoid(g).astype(BF16)

    u_ref[...] = proj(_U0, POOL_WIDTH)


def _in_proj(x2d, g_mix, w_in, b_gate, gq_t, gk_t, qshift, cos_t, shi_t, slo_t, seq,
             w_attn_up, w_out, pool_w, pool_scale, w_pool_up):
    rows = x2d.shape[0]
    tm = 1024
    steps = rows // tm
    half = N_Q_HEADS // 2
    up_rows = ATTN_WIDTH // steps
    assert HEAD_DIM % up_rows == 0
    per_head = HEAD_DIM // up_rows

    def head_pair_source(b):
        t, within = b // (2 * per_head), b % (2 * per_head)
        return ((within // per_head) * half + t) * per_head + within % per_head

    to_cast = [w_attn_up, w_out]
    cast_in, cast_out, cast_shapes = _cast_specs(to_cast, steps, source_blocks={0: head_pair_source})
    n_groups = len(POOL_WINDOWS)
    pw2d = pool_w.reshape(n_groups * POOL_GROUP, POOL_GROUP)
    ps3d = pool_scale.reshape(n_groups, 1, POOL_GROUP)
    fold_rows = POOL_WIDTH // steps
    assert POOL_GROUP % fold_rows == 0 and fold_rows % 16 == 0
    group_of = lambda i: i // (POOL_GROUP // fold_rows)
    fold_in = [pl.BlockSpec((fold_rows, POOL_GROUP), lambda i: (i, 0)),
               pl.BlockSpec((1, 1, POOL_GROUP), lambda i: (group_of(i), 0, 0)),
               pl.BlockSpec((POOL_GROUP, D_MODEL), lambda i: (group_of(i), 0))]
    fold_out = [pl.BlockSpec((fold_rows, D_MODEL), lambda i: (i, 0))]
    fold_shape = [jax.ShapeDtypeStruct((POOL_WIDTH, D_MODEL), BF16)]
    seq_tiles = seq // tm
    const = lambda i: (0, 0)
    row = lambda i: (i, 0)
    pos = lambda i: (i % seq_tiles, 0)
    in_width = w_in.shape[1]
    return pl.pallas_call(
        functools.partial(_in_proj_kernel, n_cast=len(to_cast)),
        grid=(rows // tm,),
        in_specs=[
            pl.BlockSpec((tm, D_MODEL), row),
            pl.BlockSpec((1, D_MODEL), const),
            pl.BlockSpec((D_MODEL, in_width), const, pipeline_mode=pl.Buffered(1)),
            pl.BlockSpec((1, 2 * D_MODEL), const),
            pl.BlockSpec((1, LANES), const),
            pl.BlockSpec((1, LANES), const),
            pl.BlockSpec((1, LANES), const),
            pl.BlockSpec((tm, LANES), pos),
            pl.BlockSpec((tm, LANES), pos),
            pl.BlockSpec((tm, LANES), pos),
        ] + cast_in + fold_in,
        out_specs=[
            pl.BlockSpec((tm, N_Q_HEADS * LANES), row),
            pl.BlockSpec((tm, N_KV_HEADS * LANES), row),
            pl.BlockSpec((tm, 2 * LANES), row),
            pl.BlockSpec((tm, POOL_WIDTH), row),
            pl.BlockSpec((tm, 2 * D_MODEL), row),
        ] + cast_out + fold_out,
        out_shape=[
            jax.ShapeDtypeStruct((rows, N_Q_HEADS * LANES), BF16),
            jax.ShapeDtypeStruct((rows, N_KV_HEADS * LANES), BF16),
            jax.ShapeDtypeStruct((rows, 2 * LANES), BF16),
            jax.ShapeDtypeStruct((rows, POOL_WIDTH), F32),
            jax.ShapeDtypeStruct((rows, 2 * D_MODEL), BF16),
        ] + cast_shapes + fold_shape,
        compiler_params=pltpu.CompilerParams(
            dimension_semantics=("arbitrary",), vmem_limit_bytes=VMEM_LIMIT),
        name="in_proj",
    )(x2d, g_mix, w_in, b_gate, gq_t, gk_t, qshift, cos_t, shi_t, slo_t, *to_cast,
      pw2d, ps3d, w_pool_up)


def _pooled(i, n_tiles, u_ref, uprev_ref, unext_ref, icnt_ref, slab_ref, wa_ref, wb_ref):
    tq = u_ref.shape[0]
    span = tq + 2 * POOL_HALO
    zero_halo = jnp.zeros((POOL_HALO, POOL_WIDTH), F32)
    slab_ref[0:POOL_HALO, :] = jnp.where(i > 0, uprev_ref[...], zero_halo)
    slab_ref[POOL_HALO:POOL_HALO + tq, :] = u_ref[...]
    slab_ref[POOL_HALO + tq:span, :] = jnp.where(i < n_tiles - 1, unext_ref[...], zero_halo)
    slab_ref[span:, :] = zero_halo
    wa_ref[span:, :] = jnp.zeros((POOL_HALO, 2 * POOL_GROUP), F32)
    wb_ref[span:, :] = jnp.zeros((POOL_HALO, 2 * POOL_GROUP), F32)

    g0 = slice(0, POOL_GROUP)
    g1 = slice(POOL_GROUP, 2 * POOL_GROUP)
    wide = slice(2 * POOL_GROUP, POOL_WIDTH)
    h = POOL_HALO
    win2 = slab_ref[h - 1:h - 1 + tq, g0] + slab_ref[h:h + tq, g0]
    win4 = ((slab_ref[h - 2:h - 2 + tq, g1] + slab_ref[h - 1:h - 1 + tq, g1])
            + (slab_ref[h:h + tq, g1] + slab_ref[h + 1:h + 1 + tq, g1]))
    wa_ref[0:span, :] = slab_ref[0:span, wide] + slab_ref[1:span + 1, wide]
    wb_ref[0:span, :] = wa_ref[0:span, :] + wa_ref[2:span + 2, :]
    wa_ref[0:span, :] = wb_ref[0:span, :] + wb_ref[4:span + 4, :]
    win8 = wa_ref[h - 4:h - 4 + tq, 0:POOL_GROUP]
    win16 = wa_ref[0:tq, POOL_GROUP:] + wa_ref[h:h + tq, POOL_GROUP:]

    pooled = []
    for gi, win in enumerate((win2, win4, win8, win16)):
        cols = slice(gi * POOL_GROUP, (gi + 1) * POOL_GROUP)
        pooled.append((win * icnt_ref[:, cols] - u_ref[:, cols]).astype(BF16))
    return pooled


def _mixer_kernel(q_ref, k_ref, v_ref, u_ref, uprev_ref, unext_ref, icnt_ref, gate_ref, x_ref,
                  wau_ref, wf_ref, wo_ref, *rest, tq, seq, subtract_max, n_cast):
    cast_src, h_ref, rest = rest[:n_cast], rest[n_cast], rest[n_cast + 1:]
    cast_dst, (slab_ref, wa_ref, wb_ref, attn_ref) = rest[:n_cast], rest[n_cast:]
    _cast_blocks(cast_src, cast_dst)
    step = pl.program_id(0)
    n_tiles = seq // tq
    i = jnp.maximum(step - 1, 0) % n_tiles

    @pl.when(step == 0)
    def _():
        attn_ref[...] = jnp.zeros_like(attn_ref)

    o_prev = attn_ref[...]

    pooled = _pooled(i, n_tiles, u_ref, uprev_ref, unext_ref, icnt_ref, slab_ref, wa_ref, wb_ref)

    v = v_ref[...]
    lane = lax.broadcasted_iota(jnp.int32, (tq, LANES), 1)
    first = lane < HEAD_DIM
    heads = []
    for h in range(N_Q_HEADS):
        g = h // (N_Q_HEADS // N_KV_HEADS)
        s = lax.dot_general(q_ref[:, h * LANES:(h + 1) * LANES], k_ref[:, g * LANES:(g + 1) * LANES],
                            (((1,), (1,)), ((), ())), preferred_element_type=F32)
        if subtract_max:
            s = s - jnp.max(s, axis=-1, keepdims=True)
        p = jnp.exp2(s).astype(BF16)
        r = jnp.dot(p, v, preferred_element_type=F32)
        heads.append(r[:, :LANES] / r[:, LANES:LANES + 1])
    half = N_Q_HEADS // 2
    attn_ref[...] = jnp.concatenate(
        [jnp.where(first, heads[t], heads[half + t]) for t in range(half)], axis=1).astype(BF16)
    parts = 2
    rows = tq // parts
    sl = [slice(j * rows, (j + 1) * rows) for j in range(parts)]
    pooled_all = jnp.concatenate(pooled, axis=1)
    a = [jnp.dot(o_prev[sl[j]], wau_ref[...], preferred_element_type=F32) for j in range(parts)]
    p_branch = [jnp.dot(pooled_all[sl[j]], wf_ref[...], preferred_element_type=F32)
                for j in range(parts)]
    mixed = []
    for j in range(parts):
        g_a = gate_ref[sl[j], :D_MODEL].astype(F32)
        g_p = gate_ref[sl[j], D_MODEL:].astype(F32)
        mixed.append((g_a * a[j] + g_p * p_branch[j]).astype(BF16))
    for j in range(parts):
        h_ref[sl[j], :] = x_ref[sl[j], :] + jnp.dot(mixed[j], wo_ref[...],
                                                    preferred_element_type=F32)


def _mixer(q, k, v, u, icnt, gates, x2d, wau, wf, wo, *to_cast, batch, seq, subtract_max):
    tq = 512
    n_tiles = seq // tq
    total = batch * n_tiles
    cast_in, cast_out, cast_shapes = _cast_specs(to_cast, total, lambda s: jnp.minimum(s, total - 1))
    halo_blocks = tq // POOL_HALO
    seq_halos = seq // POOL_HALO
    attn_tile = lambda s: jnp.minimum(s, total - 1)
    post_tile = lambda s: jnp.maximum(s - 1, 0)
    attn_row = lambda s: (attn_tile(s), 0)
    per_batch = lambda s: (attn_tile(s) // n_tiles, 0)
    row = lambda s: (post_tile(s), 0)
    seq_row = lambda s: (post_tile(s) % n_tiles, 0)
    const2 = lambda s: (0, 0)

    def prev(s):
        t = post_tile(s)
        return ((t // n_tiles) * seq_halos + jnp.maximum((t % n_tiles) * halo_blocks - 1, 0), 0)

    def nxt(s):
        t = post_tile(s)
        return ((t // n_tiles) * seq_halos
                + jnp.minimum((t % n_tiles + 1) * halo_blocks, seq_halos - 1), 0)

    return pl.pallas_call(
        functools.partial(_mixer_kernel, tq=tq, seq=seq, subtract_max=subtract_max,
                          n_cast=len(to_cast)),
        grid=(total + 1,),
        in_specs=[
            pl.BlockSpec((tq, N_Q_HEADS * LANES), attn_row),
            pl.BlockSpec((seq, N_KV_HEADS * LANES), per_batch),
            pl.BlockSpec((seq, 2 * LANES), per_batch),
            pl.BlockSpec((tq, POOL_WIDTH), row),
            pl.BlockSpec((POOL_HALO, POOL_WIDTH), prev),
            pl.BlockSpec((POOL_HALO, POOL_WIDTH), nxt),
            pl.BlockSpec((tq, POOL_WIDTH), seq_row),
            pl.BlockSpec((tq, 2 * D_MODEL), row),
            pl.BlockSpec((tq, D_MODEL), row),
            pl.BlockSpec((ATTN_WIDTH, D_MODEL), const2),
            pl.BlockSpec((POOL_WIDTH, D_MODEL), const2),
            pl.BlockSpec((D_MODEL, D_MODEL), const2),
        ] + cast_in,
        out_specs=[pl.BlockSpec((tq, D_MODEL), row)] + cast_out,
        out_shape=[jax.ShapeDtypeStruct(x2d.shape, F32)] + cast_shapes,
        scratch_shapes=[pltpu.VMEM((tq + 3 * POOL_HALO, POOL_WIDTH), F32),
                        pltpu.VMEM((tq + 3 * POOL_HALO, 2 * POOL_GROUP), F32),
                        pltpu.VMEM((tq + 3 * POOL_HALO, 2 * POOL_GROUP), F32),
                        pltpu.VMEM((tq, ATTN_WIDTH), BF16)],
        compiler_params=pltpu.CompilerParams(
            dimension_semantics=("arbitrary",), vmem_limit_bytes=VMEM_LIMIT),
        name="mixer_max" if subtract_max else "mixer",
    )(q, k, v, u, u, u, icnt, gates, x2d, wau, wf, wo, *to_cast)


def _cross_ffn_kernel(h_ref, xk_ref, xv_ref, gc_ref, wq_ref, wo_ref, gf_ref, w1_ref, w2_ref,
                      gfin_ref, out_ref, *, parts):
    rows = h_ref.shape[0] // parts
    sl = [slice(j * rows, (j + 1) * rows) for j in range(parts)]
    every = range(parts)

    h = [h_ref[sl[j], :] for j in every]
    n2 = [(h[j] * _rms_scale(h[j]) * gc_ref[...]).astype(BF16) for j in every]
    xq = [(jnp.dot(n2[j], wq_ref[...], preferred_element_type=F32) * (X_HEAD_DIM ** -0.5))
          .astype(BF16) for j in every]
    outs = [[] for _ in every]
    for hd in range(N_X_HEADS):
        cols = slice(hd * X_HEAD_DIM, (hd + 1) * X_HEAD_DIM)
        for j in every:
            s = lax.dot_general(xq[j][:, cols], xk_ref[:, cols], (((1,), (1,)), ((), ())),
                                preferred_element_type=F32)
            p, l = _softmax_rows(s)
            o = jnp.dot(p.astype(BF16), xv_ref[:, cols], preferred_element_type=F32)
            outs[j].append((o / l).astype(BF16))
    h = [h[j] + jnp.dot(jnp.concatenate(outs[j], axis=1), wo_ref[...], preferred_element_type=F32)
         for j in every]

    n3 = [(h[j] * _rms_scale(h[j]) * gf_ref[...]).astype(BF16) for j in every]
    chunk = 1024
    acc = h
    for c in range(D_FF // chunk):
        t = [jnp.dot(n3[j], w1_ref[:, c * chunk:(c + 1) * chunk], preferred_element_type=F32)
             for j in every]
        t = [jnp.square(jnp.maximum(t[j], 0.0)).astype(BF16) for j in every]
        acc = [acc[j] + jnp.dot(t[j], w2_ref[c * chunk:(c + 1) * chunk, :],
                                preferred_element_type=F32) for j in every]
    for j in every:
        out_ref[sl[j], :] = acc[j] * _rms_scale(acc[j]) * gfin_ref[...]


def _cross_ffn(h2d, xk, xv, g_cross, w_xq, w_xo, g_ffn, w_ff1, w_ff2, g_final, seq, n_mem):
    rows = h2d.shape[0]
    tm = 1024
    tiles_per_batch = seq // tm
    const = lambda i: (0, 0)
    row = lambda i: (i, 0)
    per_batch = lambda i: (i // tiles_per_batch, 0)
    single = pl.Buffered(1)
    return pl.pallas_call(
        functools.partial(_cross_ffn_kernel, parts=4),
        grid=(rows // tm,),
        in_specs=[
            pl.BlockSpec((tm, D_MODEL), row),
            pl.BlockSpec((n_mem, D_MODEL), per_batch),
            pl.BlockSpec((n_mem, D_MODEL), per_batch),
            pl.BlockSpec((1, D_MODEL), const),
            pl.BlockSpec((D_MODEL, D_MODEL), const, pipeline_mode=single),
            pl.BlockSpec((D_MODEL, D_MODEL), const, pipeline_mode=single),
            pl.BlockSpec((1, D_MODEL), const),
            pl.BlockSpec((D_MODEL, D_FF), const, pipeline_mode=single),
            pl.BlockSpec((D_FF, D_MODEL), const, pipeline_mode=single),
            pl.BlockSpec((1, D_MODEL), const),
        ],
        out_specs=pl.BlockSpec((tm, D_MODEL), row),
        out_shape=jax.ShapeDtypeStruct(h2d.shape, F32),
        compiler_params=pltpu.CompilerParams(
            dimension_semantics=("arbitrary",), vmem_limit_bytes=VMEM_LIMIT),
        name="cross_ffn",
    )(h2d, xk, xv, g_cross, w_xq, w_xo, g_ffn, w_ff1, w_ff2, g_final)


def _rope_tables(seq):
    t = np.arange(seq)[:, None]
    d = np.arange(LANES)[None, :] % HEAD_DIM
    freq = d % ROPE_FREQS
    upper_half = (d // ROPE_FREQS) % 2 == 1
    pos = np.where(d < HEAD_DIM // 2, t // GRID_W, t % GRID_W).astype(np.float32)
    inv = np.float32(ROPE_THETA) ** (-(2 * freq).astype(np.float32) / np.float32(2 * ROPE_FREQS))
    ang = (pos * inv).astype(np.float32)
    cos, sin = np.cos(ang).astype(np.float32), np.sin(ang).astype(np.float32)
    zero = np.float32(0.0)
    return (jnp.asarray(cos), jnp.asarray(np.where(upper_half, sin, zero)),
            jnp.asarray(np.where(upper_half, zero, -sin)))


def _pool_inverse_counts(seq):
    t = np.arange(seq)[:, None]
    w = np.repeat(np.asarray(POOL_WINDOWS), POOL_GROUP)[None, :]
    lo = np.maximum(t - w // 2, 0)
    hi = np.minimum(t + (w - w // 2), seq)
    return jnp.asarray((1.0 / (hi - lo)).astype(np.float32))


def kernel(x, mem, g_mix, w_in, b_gate, g_q, g_k, w_attn_up, pool_w, pool_scale, w_pool_up, w_out,
           g_cross, g_mem, w_xq, w_xkv, w_xo, g_ffn, w_ff1, w_ff2, g_final):
    batch, seq, d = x.shape
    n_mem = mem.shape[1]
    assert d == D_MODEL and w_in.shape[0] == 1 and seq % 1024 == 0 and (batch * n_mem) % 1024 == 0
    l = 0
    x2d = x.reshape(batch * seq, d)
    mem2d = mem.reshape(batch * n_mem, d)
    row = lambda a: a.reshape(1, -1)

    cos_t, shi_t, slo_t = _rope_tables(seq)
    two_heads = lambda g: jnp.concatenate([g, g]).reshape(1, LANES)
    gq_t = two_heads(g_q[l] * (HEAD_DIM ** -0.5 * LOG2_E))
    gk_t = two_heads(g_k[l])
    shift = 1.02 * HEAD_DIM ** 0.5 * jnp.max(jnp.abs(g_q[l])) * jnp.max(jnp.abs(g_k[l])) * LOG2_E
    fixed_shift_ok = shift <= MAX_FIXED_SHIFT * LOG2_E
    qshift = jnp.where((jnp.arange(LANES) == HEAD_DIM) & fixed_shift_ok, -shift, 0.0)
    qshift = qshift.astype(F32).reshape(1, LANES)
    xk, xv, w_in_b = _mem_kv(mem2d, row(g_mem[l]), w_xkv[l].astype(BF16), w_in[l])
    q, k, v, u, gates, wau_b, wo_b, wf_b = _in_proj(
        x2d, row(g_mix[l]), w_in_b, row(b_gate[l]), gq_t, gk_t, qshift, cos_t, shi_t, slo_t, seq,
        w_attn_up[l], w_out[l], pool_w[l], pool_scale[l], w_pool_up[l])
    mixer_args = (q, k, v, u, _pool_inverse_counts(seq), gates, x2d, wau_b, wf_b, wo_b,
                  w_xq[l], w_xo[l], w_ff1[l], w_ff2[l])
    h1, w_xq_b, w_xo_b, w_ff1_b, w_ff2_b = lax.cond(
        fixed_shift_ok,
        lambda *a: _mixer(*a, batch=batch, seq=seq, subtract_max=False),
        lambda *a: _mixer(*a, batch=batch, seq=seq, subtract_max=True),
        *mixer_args)
    out = _cross_ffn(h1, xk, xv, row(g_cross[l]), w_xq_b, w_xo_b, row(g_ffn[l]), w_ff1_b, w_ff2_b,
                     row(g_final), seq, n_mem)
    return out.reshape(batch, seq, d)
```

```python
import functools
import math

import jax
import jax.numpy as jnp
import numpy as np
from jax import lax
from jax.experimental import pallas as pl
from jax.experimental.pallas import tpu as pltpu

D_MODEL = 1024
GRID_W = 64
HEAD_DIM = 64
N_Q_HEADS = 8
N_KV_HEADS = 2
ATTN_WIDTH = N_Q_HEADS * HEAD_DIM
KV_WIDTH = N_KV_HEADS * HEAD_DIM
POOL_WINDOWS = (2, 4, 8, 16)
POOL_WIDTH = 512
POOL_GROUP = 128
ROPE_THETA = 10000.0
ROPE_FREQS = 16
N_X_HEADS = 4
X_HEAD_DIM = 256
D_FF = 4096
EPS = 1e-6

LANES = 128
POOL_HALO = 8
VMEM_LIMIT = 56 * 1024 * 1024
LOG2_E = math.log2(math.e)
MAX_FIXED_SHIFT = 40.0

BF16 = jnp.bfloat16
F32 = jnp.float32

_Q0, _K0, _V0, _U0, _G0 = 0, 512, 640, 768, 1280


def _rms_scale(x):
    return lax.rsqrt(jnp.mean(x * x, axis=-1, keepdims=True) + EPS)


def _head_rms_scale(x128):
    lane = lax.broadcasted_iota(jnp.int32, x128.shape, 1)
    first = lane < HEAD_DIM
    sq = x128 * x128
    s_a = jnp.sum(jnp.where(first, sq, 0.0), axis=-1, keepdims=True)
    s_b = jnp.sum(jnp.where(first, 0.0, sq), axis=-1, keepdims=True)
    r_a = lax.rsqrt(s_a * (1.0 / HEAD_DIM) + EPS)
    r_b = lax.rsqrt(s_b * (1.0 / HEAD_DIM) + EPS)
    return jnp.where(first, r_a, r_b)


def _rope(xn, cos, sin_hi, sin_lo):
    return (xn * cos
            + pltpu.roll(xn, ROPE_FREQS, 1) * sin_hi
            + pltpu.roll(xn, LANES - ROPE_FREQS, 1) * sin_lo)


def _softmax_rows(s):
    m = jnp.max(s, axis=-1, keepdims=True)
    p = jnp.exp(s - m)
    return p, jnp.sum(p, axis=-1, keepdims=True)


def _cast_specs(weights, n_blocks, step_to_block=lambda s: s, source_blocks=None):
    in_specs, out_specs, out_shapes = [], [], []
    for i, w in enumerate(weights):
        rows, cols = w.shape
        assert rows % (n_blocks * 16) == 0, (w.shape, n_blocks)
        source = (source_blocks or {}).get(i, lambda b: b)
        in_specs.append(pl.BlockSpec((rows // n_blocks, cols),
                                     lambda s, source=source: (source(step_to_block(s)), 0)))
        out_specs.append(pl.BlockSpec((rows // n_blocks, cols), lambda s: (step_to_block(s), 0)))
        out_shapes.append(jax.ShapeDtypeStruct(w.shape, BF16))
    return in_specs, out_specs, out_shapes


def _cast_blocks(src_refs, dst_refs):
    assert len(src_refs) == len(dst_refs)
    for src, dst in zip(src_refs, dst_refs):
        dst[...] = src[...].astype(BF16)


def _fold_pool_block(pw_ref, ps_ref, wpu_ref, wf_ref):
    scaled = pw_ref[...] * ps_ref[0]
    wf_ref[...] = jnp.dot(scaled, wpu_ref[...], preferred_element_type=F32,
                          precision=lax.Precision.HIGHEST).astype(BF16)


def _mem_kv_kernel(mem_ref, g_ref, w_ref, cast_src, xk_ref, xv_ref, cast_dst):
    _cast_blocks([cast_src], [cast_dst])
    m = mem_ref[...]
    mn = (m * _rms_scale(m) * g_ref[...]).astype(BF16)
    kv = jnp.dot(mn, w_ref[...], preferred_element_type=F32)
    xk_ref[...] = kv[:, :D_MODEL].astype(BF16)
    xv_ref[...] = kv[:, D_MODEL:].astype(BF16)


def _mem_kv(mem2d, g_mem, w_xkv, w_to_cast):
    rows = mem2d.shape[0]
    tm = 1024
    steps = rows // tm
    const = lambda i: (0, 0)
    cast_in, cast_out, cast_shapes = _cast_specs([w_to_cast], steps)
    return pl.pallas_call(
        _mem_kv_kernel,
        grid=(steps,),
        in_specs=[
            pl.BlockSpec((tm, D_MODEL), lambda i: (i, 0)),
            pl.BlockSpec((1, D_MODEL), const),
            pl.BlockSpec((D_MODEL, 2 * D_MODEL), const),
        ] + cast_in,
        out_specs=[
            pl.BlockSpec((tm, D_MODEL), lambda i: (i, 0)),
            pl.BlockSpec((tm, D_MODEL), lambda i: (i, 0)),
        ] + cast_out,
        out_shape=[jax.ShapeDtypeStruct((rows, D_MODEL), BF16)] * 2 + cast_shapes,
        compiler_params=pltpu.CompilerParams(
            dimension_semantics=("arbitrary",), vmem_limit_bytes=VMEM_LIMIT),
        name="mem_kv",
    )(mem2d, g_mem, w_xkv, w_to_cast)


def _in_proj_kernel(x_ref, gmix_ref, w_ref, bg_ref, gq_ref, gk_ref, qshift_ref,
                    cos_ref, shi_ref, slo_ref, *rest, n_cast, parts):
    cast_src, (pw_ref, ps_ref, wpu_ref), rest = rest[:n_cast], rest[n_cast:n_cast + 3], rest[n_cast + 3:]
    (q_ref, k_ref, v_ref, u_ref, gate_ref), cast_dst, wf_ref = rest[:5], rest[5:5 + n_cast], rest[5 + n_cast]
    _cast_blocks(cast_src, cast_dst)
    _fold_pool_block(pw_ref, ps_ref, wpu_ref, wf_ref)
    rows = x_ref.shape[0] // parts
    sl = [slice(j * rows, (j + 1) * rows) for j in range(parts)]
    every = range(parts)
    lane = lax.broadcasted_iota(jnp.int32, (rows, LANES), 1)
    first = lane < HEAD_DIM
    gq, gk, qshift = gq_ref[...], gk_ref[...], qshift_ref[...]
    one_at_head_dim = jnp.where(lane == HEAD_DIM, 1.0, 0.0)
    one_at_lane_0 = jnp.where(lane == 0, 1.0, 0.0).astype(BF16)

    n1 = []
    for j in every:
        x = x_ref[sl[j], :]
        n1.append((x * _rms_scale(x) * gmix_ref[...]).astype(BF16))

    def proj(j, c0, width):
        return jnp.dot(n1[j], w_ref[:, c0:c0 + width], preferred_element_type=F32)

    def rope(j, xn):
        return _rope(xn, cos_ref[sl[j], :], shi_ref[sl[j], :], slo_ref[sl[j], :])

    q = [proj(j, _Q0, ATTN_WIDTH) for j in every]
    for j in every:
        for t in range(ATTN_WIDTH // LANES):
            qt = q[j][:, t * LANES:(t + 1) * LANES]
            qt = rope(j, qt * _head_rms_scale(qt) * gq)
            q_ref[sl[j], (2 * t) * LANES:(2 * t + 1) * LANES] = (
                jnp.where(first, qt, qshift).astype(BF16))
            q_ref[sl[j], (2 * t + 1) * LANES:(2 * t + 2) * LANES] = jnp.where(
                first, pltpu.roll(qt, HEAD_DIM, 1), qshift).astype(BF16)

    kv = [proj(j, _K0, 2 * KV_WIDTH) for j in every]
    for j in every:
        kt = kv[j][:, :KV_WIDTH]
        kt = rope(j, kt * _head_rms_scale(kt) * gk)
        k_ref[sl[j], :LANES] = jnp.where(first, kt, one_at_head_dim).astype(BF16)
        k_ref[sl[j], LANES:] = jnp.where(
            first, pltpu.roll(kt, HEAD_DIM, 1), one_at_head_dim).astype(BF16)
        v_ref[sl[j], :LANES] = kv[j][:, KV_WIDTH:].astype(BF16)
        v_ref[sl[j], LANES:] = one_at_lane_0

    chunk = 512
    for c in range(2 * D_MODEL // chunk):
        cols = slice(c * chunk, (c + 1) * chunk)
        g = [proj(j, _G0 + c * chunk, chunk) + bg_ref[:, cols] for j in every]
        for j in every:
            gate_ref[sl[j], cols] = jax.nn.sigmoid(g[j]).astype(BF16)

    for j in every:
        u_ref[sl[j], :] = proj(j, _U0, POOL_WIDTH)


def _in_proj(x2d, g_mix, w_in, b_gate, gq_t, gk_t, qshift, cos_t, shi_t, slo_t, seq,
             w_attn_up, w_out, pool_w, pool_scale, w_pool_up):
    rows = x2d.shape[0]
    tm = 1024
    steps = rows // tm
    half = N_Q_HEADS // 2
    up_rows = ATTN_WIDTH // steps
    assert HEAD_DIM % up_rows == 0
    per_head = HEAD_DIM // up_rows

    def head_pair_source(b):
        t, within = b // (2 * per_head), b % (2 * per_head)
        return ((within // per_head) * half + t) * per_head + within % per_head

    to_cast = [w_attn_up, w_out]
    cast_in, cast_out, cast_shapes = _cast_specs(to_cast, steps, source_blocks={0: head_pair_source})
    n_groups = len(POOL_WINDOWS)
    pw2d = pool_w.reshape(n_groups * POOL_GROUP, POOL_GROUP)
    ps3d = pool_scale.reshape(n_groups, 1, POOL_GROUP)
    fold_rows = POOL_WIDTH // steps
    assert POOL_GROUP % fold_rows == 0 and fold_rows % 16 == 0
    group_of = lambda i: i // (POOL_GROUP // fold_rows)
    fold_in = [pl.BlockSpec((fold_rows, POOL_GROUP), lambda i: (i, 0)),
               pl.BlockSpec((1, 1, POOL_GROUP), lambda i: (group_of(i), 0, 0)),
               pl.BlockSpec((POOL_GROUP, D_MODEL), lambda i: (group_of(i), 0))]
    fold_out = [pl.BlockSpec((fold_rows, D_MODEL), lambda i: (i, 0))]
    fold_shape = [jax.ShapeDtypeStruct((POOL_WIDTH, D_MODEL), BF16)]
    seq_tiles = seq // tm
    const = lambda i: (0, 0)
    row = lambda i: (i, 0)
    pos = lambda i: (i % seq_tiles, 0)
    in_width = w_in.shape[1]
    return pl.pallas_call(
        functools.partial(_in_proj_kernel, n_cast=len(to_cast), parts=4),
        grid=(rows // tm,),
        in_specs=[
            pl.BlockSpec((tm, D_MODEL), row),
            pl.BlockSpec((1, D_MODEL), const),
            pl.BlockSpec((D_MODEL, in_width), const, pipeline_mode=pl.Buffered(1)),
            pl.BlockSpec((1, 2 * D_MODEL), const),
            pl.BlockSpec((1, LANES), const),
            pl.BlockSpec((1, LANES), const),
            pl.BlockSpec((1, LANES), const),
            pl.BlockSpec((tm, LANES), pos),
            pl.BlockSpec((tm, LANES), pos),
            pl.BlockSpec((tm, LANES), pos),
        ] + cast_in + fold_in,
        out_specs=[
            pl.BlockSpec((tm, N_Q_HEADS * LANES), row),
            pl.BlockSpec((tm, N_KV_HEADS * LANES), row),
            pl.BlockSpec((tm, 2 * LANES), row),
            pl.BlockSpec((tm, POOL_WIDTH), row),
            pl.BlockSpec((tm, 2 * D_MODEL), row),
        ] + cast_out + fold_out,
        out_shape=[
            jax.ShapeDtypeStruct((rows, N_Q_HEADS * LANES), BF16),
            jax.ShapeDtypeStruct((rows, N_KV_HEADS * LANES), BF16),
            jax.ShapeDtypeStruct((rows, 2 * LANES), BF16),
            jax.ShapeDtypeStruct((rows, POOL_WIDTH), F32),
            jax.ShapeDtypeStruct((rows, 2 * D_MODEL), BF16),
        ] + cast_shapes + fold_shape,
        compiler_params=pltpu.CompilerParams(
            dimension_semantics=("arbitrary",), vmem_limit_bytes=VMEM_LIMIT),
        name="in_proj",
    )(x2d, g_mix, w_in, b_gate, gq_t, gk_t, qshift, cos_t, shi_t, slo_t, *to_cast,
      pw2d, ps3d, w_pool_up)


def _pooled(i, n_tiles, u_ref, uprev_ref, unext_ref, icnt_ref, slab_ref, wa_ref, wb_ref):
    tq = u_ref.shape[0]
    span = tq + 2 * POOL_HALO
    zero_halo = jnp.zeros((POOL_HALO, POOL_WIDTH), F32)
    slab_ref[0:POOL_HALO, :] = jnp.where(i > 0, uprev_ref[...], zero_halo)
    slab_ref[POOL_HALO:POOL_HALO + tq, :] = u_ref[...]
    slab_ref[POOL_HALO + tq:span, :] = jnp.where(i < n_tiles - 1, unext_ref[...], zero_halo)
    slab_ref[span:, :] = zero_halo
    wa_ref[span:, :] = jnp.zeros((POOL_HALO, 2 * POOL_GROUP), F32)
    wb_ref[span:, :] = jnp.zeros((POOL_HALO, 2 * POOL_GROUP), F32)

    g0 = slice(0, POOL_GROUP)
    g1 = slice(POOL_GROUP, 2 * POOL_GROUP)
    wide = slice(2 * POOL_GROUP, POOL_WIDTH)
    h = POOL_HALO
    win2 = slab_ref[h - 1:h - 1 + tq, g0] + slab_ref[h:h + tq, g0]
    win4 = ((slab_ref[h - 2:h - 2 + tq, g1] + slab_ref[h - 1:h - 1 + tq, g1])
            + (slab_ref[h:h + tq, g1] + slab_ref[h + 1:h + 1 + tq, g1]))
    wa_ref[0:span, :] = slab_ref[0:span, wide] + slab_ref[1:span + 1, wide]
    wb_ref[0:span, :] = wa_ref[0:span, :] + wa_ref[2:span + 2, :]
    wa_ref[0:span, :] = wb_ref[0:span, :] + wb_ref[4:span + 4, :]
    win8 = wa_ref[h - 4:h - 4 + tq, 0:POOL_GROUP]
    win16 = wa_ref[0:tq, POOL_GROUP:] + wa_ref[h:h + tq, POOL_GROUP:]

    for gi, win in enumerate((win2, win4, win8, win16)):
        cols = slice(gi * POOL_GROUP, (gi + 1) * POOL_GROUP)
        slab_ref[h:h + tq, cols] = win * icnt_ref[:, cols] - u_ref[:, cols]
    return slab_ref[h:h + tq, :].astype(BF16)


def _mixer_kernel(q_ref, k_ref, v_ref, u_ref, uprev_ref, unext_ref, icnt_ref, gate_ref, x_ref,
                  wau_ref, wf_ref, wo_ref, *rest, tq, seq, subtract_max, n_cast):
    cast_src, h_ref, rest = rest[:n_cast], rest[n_cast], rest[n_cast + 1:]
    cast_dst, (slab_ref, wa_ref, wb_ref, attn_ref) = rest[:n_cast], rest[n_cast:]
    _cast_blocks(cast_src, cast_dst)
    step = pl.program_id(0)
    n_tiles = seq // tq
    i = jnp.maximum(step - 1, 0) % n_tiles

    @pl.when(step == 0)
    def _():
        attn_ref[...] = jnp.zeros_like(attn_ref)

    o_prev = attn_ref[...]

    pooled = _pooled(i, n_tiles, u_ref, uprev_ref, unext_ref, icnt_ref, slab_ref, wa_ref, wb_ref)

    v = v_ref[...]
    lane = lax.broadcasted_iota(jnp.int32, (tq, LANES), 1)
    first = lane < HEAD_DIM
    heads = []
    for h in range(N_Q_HEADS):
        g = h // (N_Q_HEADS // N_KV_HEADS)
        s = lax.dot_general(q_ref[:, h * LANES:(h + 1) * LANES], k_ref[:, g * LANES:(g + 1) * LANES],
                            (((1,), (1,)), ((), ())), preferred_element_type=F32)
        if subtract_max:
            s = s - jnp.max(s, axis=-1, keepdims=True)
        p = jnp.exp2(s).astype(BF16)
        r = jnp.dot(p, v, preferred_element_type=F32)
        heads.append(r[:, :LANES] / r[:, LANES:LANES + 1])
    half = N_Q_HEADS // 2
    attn_ref[...] = jnp.concatenate(
        [jnp.where(first, heads[t], heads[half + t]) for t in range(half)], axis=1).astype(BF16)
    parts = 2
    rows = tq // parts
    sl = [slice(j * rows, (j + 1) * rows) for j in range(parts)]
    pooled_all = pooled
    a = [jnp.dot(o_prev[sl[j]], wau_ref[...], preferred_element_type=F32) for j in range(parts)]
    p_branch = [jnp.dot(pooled_all[sl[j]], wf_ref[...], preferred_element_type=F32)
                for j in range(parts)]
    mixed = []
    for j in range(parts):
        g_a = gate_ref[sl[j], :D_MODEL].astype(F32)
        g_p = gate_ref[sl[j], D_MODEL:].astype(F32)
        mixed.append((g_a * a[j] + g_p * p_branch[j]).astype(BF16))
    for j in range(parts):
        h_ref[sl[j], :] = x_ref[sl[j], :] + jnp.dot(mixed[j], wo_ref[...],
                                                    preferred_element_type=F32)


def _mixer(q, k, v, u, icnt, gates, x2d, wau, wf, wo, *to_cast, batch, seq, subtract_max):
    tq = 512
    n_tiles = seq // tq
    total = batch * n_tiles
    cast_in, cast_out, cast_shapes = _cast_specs(to_cast, total, lambda s: jnp.minimum(s, total - 1))
    halo_blocks = tq // POOL_HALO
    seq_halos = seq // POOL_HALO
    attn_tile = lambda s: jnp.minimum(s, total - 1)
    post_tile = lambda s: jnp.maximum(s - 1, 0)
    attn_row = lambda s: (attn_tile(s), 0)
    per_batch = lambda s: (attn_tile(s) // n_tiles, 0)
    row = lambda s: (post_tile(s), 0)
    seq_row = lambda s: (post_tile(s) % n_tiles, 0)
    const2 = lambda s: (0, 0)

    def prev(s):
        t = post_tile(s)
        return ((t // n_tiles) * seq_halos + jnp.maximum((t % n_tiles) * halo_blocks - 1, 0), 0)

    def nxt(s):
        t = post_tile(s)
        return ((t // n_tiles) * seq_halos
                + jnp.minimum((t % n_tiles + 1) * halo_blocks, seq_halos - 1), 0)

    return pl.pallas_call(
        functools.partial(_mixer_kernel, tq=tq, seq=seq, subtract_max=subtract_max,
                          n_cast=len(to_cast)),
        grid=(total + 1,),
        in_specs=[
            pl.BlockSpec((tq, N_Q_HEADS * LANES), attn_row),
            pl.BlockSpec((seq, N_KV_HEADS * LANES), per_batch),
            pl.BlockSpec((seq, 2 * LANES), per_batch),
            pl.BlockSpec((tq, POOL_WIDTH), row),
            pl.BlockSpec((POOL_HALO, POOL_WIDTH), prev),
            pl.BlockSpec((POOL_HALO, POOL_WIDTH), nxt),
            pl.BlockSpec((tq, POOL_WIDTH), seq_row),
            pl.BlockSpec((tq, 2 * D_MODEL), row),
            pl.BlockSpec((tq, D_MODEL), row),
            pl.BlockSpec((ATTN_WIDTH, D_MODEL), const2),
            pl.BlockSpec((POOL_WIDTH, D_MODEL), const2),
            pl.BlockSpec((D_MODEL, D_MODEL), const2),
        ] + cast_in,
        out_specs=[pl.BlockSpec((tq, D_MODEL), row)] + cast_out,
        out_shape=[jax.ShapeDtypeStruct(x2d.shape, F32)] + cast_shapes,
        scratch_shapes=[pltpu.VMEM((tq + 3 * POOL_HALO, POOL_WIDTH), F32),
                        pltpu.VMEM((tq + 3 * POOL_HALO, 2 * POOL_GROUP), F32),
                        pltpu.VMEM((tq + 3 * POOL_HALO, 2 * POOL_GROUP), F32),
                        pltpu.VMEM((tq, ATTN_WIDTH), BF16)],
        compiler_params=pltpu.CompilerParams(
            dimension_semantics=("arbitrary",), vmem_limit_bytes=VMEM_LIMIT),
        name="mixer_max" if subtract_max else "mixer",
    )(q, k, v, u, u, u, icnt, gates, x2d, wau, wf, wo, *to_cast)


def _cross_ffn_kernel(h_ref, xk_ref, xv_ref, gc_ref, wq_ref, wo_ref, gf_ref, w1_ref, w2_ref,
                      gfin_ref, out_ref, *, parts):
    rows = h_ref.shape[0] // parts
    sl = [slice(j * rows, (j + 1) * rows) for j in range(parts)]
    every = range(parts)

    h = [h_ref[sl[j], :] for j in every]
    n2 = [(h[j] * _rms_scale(h[j]) * gc_ref[...]).astype(BF16) for j in every]
    xq = [(jnp.dot(n2[j], wq_ref[...], preferred_element_type=F32) * (X_HEAD_DIM ** -0.5))
          .astype(BF16) for j in every]
    outs = [[] for _ in every]
    for hd in range(N_X_HEADS):
        cols = slice(hd * X_HEAD_DIM, (hd + 1) * X_HEAD_DIM)
        for j in every:
            s = lax.dot_general(xq[j][:, cols], xk_ref[:, cols], (((1,), (1,)), ((), ())),
                                preferred_element_type=F32)
            p, l = _softmax_rows(s)
            o = jnp.dot(p.astype(BF16), xv_ref[:, cols], preferred_element_type=F32)
            outs[j].append((o / l).astype(BF16))
    h = [h[j] + jnp.dot(jnp.concatenate(outs[j], axis=1), wo_ref[...], preferred_element_type=F32)
         for j in every]

    n3 = [(h[j] * _rms_scale(h[j]) * gf_ref[...]).astype(BF16) for j in every]
    chunk = 1024
    acc = h
    for c in range(D_FF // chunk):
        t = [jnp.dot(n3[j], w1_ref[:, c * chunk:(c + 1) * chunk], preferred_element_type=F32)
             for j in every]
        t = [jnp.square(jnp.maximum(t[j], 0.0)).astype(BF16) for j in every]
        acc = [acc[j] + jnp.dot(t[j], w2_ref[c * chunk:(c + 1) * chunk, :],
                                preferred_element_type=F32) for j in every]
    for j in every:
        out_ref[sl[j], :] = acc[j] * _rms_scale(acc[j]) * gfin_ref[...]


def _cross_ffn(h2d, xk, xv, g_cross, w_xq, w_xo, g_ffn, w_ff1, w_ff2, g_final, seq, n_mem):
    rows = h2d.shape[0]
    tm = 1024
    tiles_per_batch = seq // tm
    const = lambda i: (0, 0)
    row = lambda i: (i, 0)
    per_batch = lambda i: (i // tiles_per_batch, 0)
    single = pl.Buffered(1)
    return pl.pallas_call(
        functools.partial(_cross_ffn_kernel, parts=4),
        grid=(rows // tm,),
        in_specs=[
            pl.BlockSpec((tm, D_MODEL), row),
            pl.BlockSpec((n_mem, D_MODEL), per_batch),
            pl.BlockSpec((n_mem, D_MODEL), per_batch),
            pl.BlockSpec((1, D_MODEL), const),
            pl.BlockSpec((D_MODEL, D_MODEL), const, pipeline_mode=single),
            pl.BlockSpec((D_MODEL, D_MODEL), const, pipeline_mode=single),
            pl.BlockSpec((1, D_MODEL), const),
            pl.BlockSpec((D_MODEL, D_FF), const, pipeline_mode=single),
            pl.BlockSpec((D_FF, D_MODEL), const, pipeline_mode=single),
            pl.BlockSpec((1, D_MODEL), const),
        ],
        out_specs=pl.BlockSpec((tm, D_MODEL), row),
        out_shape=jax.ShapeDtypeStruct(h2d.shape, F32),
        compiler_params=pltpu.CompilerParams(
            dimension_semantics=("arbitrary",), vmem_limit_bytes=VMEM_LIMIT),
        name="cross_ffn",
    )(h2d, xk, xv, g_cross, w_xq, w_xo, g_ffn, w_ff1, w_ff2, g_final)


def _rope_tables(seq):
    t = np.arange(seq)[:, None]
    d = np.arange(LANES)[None, :] % HEAD_DIM
    freq = d % ROPE_FREQS
    upper_half = (d // ROPE_FREQS) % 2 == 1
    pos = np.where(d < HEAD_DIM // 2, t // GRID_W, t % GRID_W).astype(np.float32)
    inv = np.float32(ROPE_THETA) ** (-(2 * freq).astype(np.float32) / np.float32(2 * ROPE_FREQS))
    ang = (pos * inv).astype(np.float32)
    cos, sin = np.cos(ang).astype(np.float32), np.sin(ang).astype(np.float32)
    zero = np.float32(0.0)
    return (jnp.asarray(cos), jnp.asarray(np.where(upper_half, sin, zero)),
            jnp.asarray(np.where(upper_half, zero, -sin)))


def _pool_inverse_counts(seq):
    t = np.arange(seq)[:, None]
    w = np.repeat(np.asarray(POOL_WINDOWS), POOL_GROUP)[None, :]
    lo = np.maximum(t - w // 2, 0)
    hi = np.minimum(t + (w - w // 2), seq)
    return jnp.asarray((1.0 / (hi - lo)).astype(np.float32))


def kernel(x, mem, g_mix, w_in, b_gate, g_q, g_k, w_attn_up, pool_w, pool_scale, w_pool_up, w_out,
           g_cross, g_mem, w_xq, w_xkv, w_xo, g_ffn, w_ff1, w_ff2, g_final):
    batch, seq, d = x.shape
    n_mem = mem.shape[1]
    assert d == D_MODEL and w_in.shape[0] == 1 and seq % 1024 == 0 and (batch * n_mem) % 1024 == 0
    l = 0
    x2d = x.reshape(batch * seq, d)
    mem2d = mem.reshape(batch * n_mem, d)
    row = lambda a: a.reshape(1, -1)

    cos_t, shi_t, slo_t = _rope_tables(seq)
    two_heads = lambda g: jnp.concatenate([g, g]).reshape(1, LANES)
    gq_t = two_heads(g_q[l] * (HEAD_DIM ** -0.5 * LOG2_E))
    gk_t = two_heads(g_k[l])
    shift = 1.02 * HEAD_DIM ** 0.5 * jnp.max(jnp.abs(g_q[l])) * jnp.max(jnp.abs(g_k[l])) * LOG2_E
    fixed_shift_ok = shift <= MAX_FIXED_SHIFT * LOG2_E
    qshift = jnp.where((jnp.arange(LANES) == HEAD_DIM) & fixed_shift_ok, -shift, 0.0)
    qshift = qshift.astype(F32).reshape(1, LANES)
    xk, xv, w_in_b = _mem_kv(mem2d, row(g_mem[l]), w_xkv[l].astype(BF16), w_in[l])
    q, k, v, u, gates, wau_b, wo_b, wf_b = _in_proj(
        x2d, row(g_mix[l]), w_in_b, row(b_gate[l]), gq_t, gk_t, qshift, cos_t, shi_t, slo_t, seq,
        w_attn_up[l], w_out[l], pool_w[l], pool_scale[l], w_pool_up[l])
    mixer_args = (q, k, v, u, _pool_inverse_counts(seq), gates, x2d, wau_b, wf_b, wo_b,
                  w_xq[l], w_xo[l], w_ff1[l], w_ff2[l])
    h1, w_xq_b, w_xo_b, w_ff1_b, w_ff2_b = lax.cond(
        fixed_shift_ok,
        lambda *a: _mixer(*a, batch=batch, seq=seq, subtract_max=False),
        lambda *a: _mixer(*a, batch=batch, seq=seq, subtract_max=True),
        *mixer_args)
    out = _cross_ffn(h1, xk, xv, row(g_cross[l]), w_xq_b, w_xo_b, row(g_ffn[l]), w_ff1_b, w_ff2_b,
                     row(g_final), seq, n_mem)
    return out.reshape(batch, seq, d)
```

```python
import functools
import math

import jax
import jax.numpy as jnp
import numpy as np
from jax import lax
from jax.experimental import pallas as pl
from jax.experimental.pallas import tpu as pltpu

D_MODEL = 1024
GRID_W = 64
HEAD_DIM = 64
N_Q_HEADS = 8
N_KV_HEADS = 2
ATTN_WIDTH = N_Q_HEADS * HEAD_DIM
KV_WIDTH = N_KV_HEADS * HEAD_DIM
POOL_WINDOWS = (2, 4, 8, 16)
POOL_WIDTH = 512
POOL_GROUP = 128
ROPE_THETA = 10000.0
ROPE_FREQS = 16
N_X_HEADS = 4
X_HEAD_DIM = 256
D_FF = 4096
EPS = 1e-6

LANES = 128
POOL_HALO = 8
VMEM_LIMIT = 56 * 1024 * 1024
LOG2_E = math.log2(math.e)
MAX_FIXED_SHIFT = 40.0

BF16 = jnp.bfloat16
F32 = jnp.float32

_Q0, _K0, _V0, _U0, _G0 = 0, 512, 640, 768, 1280


def _rms_scale(x):
    return lax.rsqrt(jnp.mean(x * x, axis=-1, keepdims=True) + EPS)


def _head_rms_scale(x128):
    lane = lax.broadcasted_iota(jnp.int32, x128.shape, 1)
    first = lane < HEAD_DIM
    sq = x128 * x128
    s_a = jnp.sum(jnp.where(first, sq, 0.0), axis=-1, keepdims=True)
    s_b = jnp.sum(jnp.where(first, 0.0, sq), axis=-1, keepdims=True)
    return lax.rsqrt(jnp.where(first, s_a, s_b) * (1.0 / HEAD_DIM) + EPS)


def _rope(xn, cos, sin_hi, sin_lo):
    return (xn * cos
            + pltpu.roll(xn, ROPE_FREQS, 1) * sin_hi
            + pltpu.roll(xn, LANES - ROPE_FREQS, 1) * sin_lo)


def _softmax_rows(s):
    m = jnp.max(s, axis=-1, keepdims=True)
    p = jnp.exp(s - m)
    return p, jnp.sum(p, axis=-1, keepdims=True)


def _cast_specs(weights, n_blocks, step_to_block=lambda s: s, source_blocks=None):
    in_specs, out_specs, out_shapes = [], [], []
    for i, w in enumerate(weights):
        rows, cols = w.shape
        assert rows % (n_blocks * 16) == 0, (w.shape, n_blocks)
        source = (source_blocks or {}).get(i, lambda b: b)
        in_specs.append(pl.BlockSpec((rows // n_blocks, cols),
                                     lambda s, source=source: (source(step_to_block(s)), 0)))
        out_specs.append(pl.BlockSpec((rows // n_blocks, cols), lambda s: (step_to_block(s), 0)))
        out_shapes.append(jax.ShapeDtypeStruct(w.shape, BF16))
    return in_specs, out_specs, out_shapes


def _cast_blocks(src_refs, dst_refs):
    assert len(src_refs) == len(dst_refs)
    for src, dst in zip(src_refs, dst_refs):
        dst[...] = src[...].astype(BF16)


def _fold_pool_block(pw_ref, ps_ref, wpu_ref, wf_ref):
    scaled = pw_ref[...] * ps_ref[0]
    wf_ref[...] = jnp.dot(scaled, wpu_ref[...], preferred_element_type=F32,
                          precision=lax.Precision.HIGHEST).astype(BF16)


def _mem_kv_kernel(mem_ref, g_ref, w_ref, cast_src, xk_ref, xv_ref, cast_dst):
    _cast_blocks([cast_src], [cast_dst])
    m = mem_ref[...]
    mn = (m * _rms_scale(m) * g_ref[...]).astype(BF16)
    kv = jnp.dot(mn, w_ref[...].astype(BF16), preferred_element_type=F32)
    xk_ref[...] = kv[:, :D_MODEL].astype(BF16)
    xv_ref[...] = kv[:, D_MODEL:].astype(BF16)


def _mem_kv(mem2d, g_mem, w_xkv, w_to_cast):
    rows = mem2d.shape[0]
    tm = 1024
    steps = rows // tm
    const = lambda i: (0, 0)
    cast_in, cast_out, cast_shapes = _cast_specs([w_to_cast], steps)
    return pl.pallas_call(
        _mem_kv_kernel,
        grid=(steps,),
        in_specs=[
            pl.BlockSpec((tm, D_MODEL), lambda i: (i, 0)),
            pl.BlockSpec((1, D_MODEL), const),
            pl.BlockSpec((D_MODEL, 2 * D_MODEL), const, pipeline_mode=pl.Buffered(1)),
        ] + cast_in,
        out_specs=[
            pl.BlockSpec((tm, D_MODEL), lambda i: (i, 0)),
            pl.BlockSpec((tm, D_MODEL), lambda i: (i, 0)),
        ] + cast_out,
        out_shape=[jax.ShapeDtypeStruct((rows, D_MODEL), BF16)] * 2 + cast_shapes,
        compiler_params=pltpu.CompilerParams(
            dimension_semantics=("arbitrary",), vmem_limit_bytes=VMEM_LIMIT),
        name="mem_kv",
    )(mem2d, g_mem, w_xkv, w_to_cast)


def _in_proj_kernel(x_ref, gmix_ref, w_ref, bg_ref, gq_ref, gk_ref, qshift_ref,
                    cos_ref, shi_ref, slo_ref, *rest, n_cast, parts):
    cast_src, (pw_ref, ps_ref, wpu_ref), rest = rest[:n_cast], rest[n_cast:n_cast + 3], rest[n_cast + 3:]
    (q_ref, k_ref, v_ref, u_ref, gate_ref), cast_dst, wf_ref = rest[:5], rest[5:5 + n_cast], rest[5 + n_cast]
    _cast_blocks(cast_src, cast_dst)
    _fold_pool_block(pw_ref, ps_ref, wpu_ref, wf_ref)
    rows = x_ref.shape[0] // parts
    sl = [slice(j * rows, (j + 1) * rows) for j in range(parts)]
    every = range(parts)
    lane = lax.broadcasted_iota(jnp.int32, (rows, LANES), 1)
    first = lane < HEAD_DIM
    gq, gk, qshift = gq_ref[...], gk_ref[...], qshift_ref[...]
    one_at_head_dim = jnp.where(lane == HEAD_DIM, 1.0, 0.0)
    one_at_lane_0 = jnp.where(lane == 0, 1.0, 0.0).astype(BF16)

    n1 = []
    for j in every:
        x = x_ref[sl[j], :]
        n1.append((x * _rms_scale(x) * gmix_ref[...]).astype(BF16))

    def proj(j, c0, width):
        return jnp.dot(n1[j], w_ref[:, c0:c0 + width], preferred_element_type=F32)

    def rope(j, xn):
        return _rope(xn, cos_ref[sl[j], :], shi_ref[sl[j], :], slo_ref[sl[j], :])

    q = [proj(j, _Q0, ATTN_WIDTH) for j in every]
    for j in every:
        for t in range(ATTN_WIDTH // LANES):
            qt = q[j][:, t * LANES:(t + 1) * LANES]
            qt = rope(j, qt * _head_rms_scale(qt) * gq)
            q_ref[sl[j], (2 * t) * LANES:(2 * t + 1) * LANES] = (
                jnp.where(first, qt, qshift).astype(BF16))
            q_ref[sl[j], (2 * t + 1) * LANES:(2 * t + 2) * LANES] = jnp.where(
                first, pltpu.roll(qt, HEAD_DIM, 1), qshift).astype(BF16)

    kv = [proj(j, _K0, 2 * KV_WIDTH) for j in every]
    for j in every:
        kt = kv[j][:, :KV_WIDTH]
        kt = rope(j, kt * _head_rms_scale(kt) * gk)
        k_ref[sl[j], :LANES] = jnp.where(first, kt, one_at_head_dim).astype(BF16)
        k_ref[sl[j], LANES:] = jnp.where(
            first, pltpu.roll(kt, HEAD_DIM, 1), one_at_head_dim).astype(BF16)
        v_ref[sl[j], :LANES] = kv[j][:, KV_WIDTH:].astype(BF16)
        v_ref[sl[j], LANES:] = one_at_lane_0

    chunk = 512
    for c in range(2 * D_MODEL // chunk):
        cols = slice(c * chunk, (c + 1) * chunk)
        g = [proj(j, _G0 + c * chunk, chunk) + bg_ref[:, cols] for j in every]
        for j in every:
            gate_ref[sl[j], cols] = (0.5 * jnp.tanh(0.5 * g[j]) + 0.5).astype(BF16)

    for j in every:
        u_ref[sl[j], :] = proj(j, _U0, POOL_WIDTH)


def _in_proj(x2d, g_mix, w_in, b_gate, gq_t, gk_t, qshift, cos_t, shi_t, slo_t, seq,
             w_attn_up, w_out, pool_w, pool_scale, w_pool_up):
    rows = x2d.shape[0]
    tm = 1024
    steps = rows // tm
    half = N_Q_HEADS // 2
    up_rows = ATTN_WIDTH // steps
    assert HEAD_DIM % up_rows == 0
    per_head = HEAD_DIM // up_rows

    def head_pair_source(b):
        t, within = b // (2 * per_head), b % (2 * per_head)
        return ((within // per_head) * half + t) * per_head + within % per_head

    to_cast = [w_attn_up, w_out]
    cast_in, cast_out, cast_shapes = _cast_specs(to_cast, steps, source_blocks={0: head_pair_source})
    n_groups = len(POOL_WINDOWS)
    pw2d = pool_w.reshape(n_groups * POOL_GROUP, POOL_GROUP)
    ps3d = pool_scale.reshape(n_groups, 1, POOL_GROUP)
    fold_rows = POOL_WIDTH // steps
    assert POOL_GROUP % fold_rows == 0 and fold_rows % 16 == 0
    group_of = lambda i: i // (POOL_GROUP // fold_rows)
    fold_in = [pl.BlockSpec((fold_rows, POOL_GROUP), lambda i: (i, 0)),
               pl.BlockSpec((1, 1, POOL_GROUP), lambda i: (group_of(i), 0, 0)),
               pl.BlockSpec((POOL_GROUP, D_MODEL), lambda i: (group_of(i), 0))]
    fold_out = [pl.BlockSpec((fold_rows, D_MODEL), lambda i: (i, 0))]
    fold_shape = [jax.ShapeDtypeStruct((POOL_WIDTH, D_MODEL), BF16)]
    seq_tiles = seq // tm
    const = lambda i: (0, 0)
    row = lambda i: (i, 0)
    pos = lambda i: (i % seq_tiles, 0)
    in_width = w_in.shape[1]
    return pl.pallas_call(
        functools.partial(_in_proj_kernel, n_cast=len(to_cast), parts=4),
        grid=(rows // tm,),
        in_specs=[
            pl.BlockSpec((tm, D_MODEL), row),
            pl.BlockSpec((1, D_MODEL), const),
            pl.BlockSpec((D_MODEL, in_width), const, pipeline_mode=pl.Buffered(1)),
            pl.BlockSpec((1, 2 * D_MODEL), const),
            pl.BlockSpec((1, LANES), const),
            pl.BlockSpec((1, LANES), const),
            pl.BlockSpec((1, LANES), const),
            pl.BlockSpec((tm, LANES), pos),
            pl.BlockSpec((tm, LANES), pos),
            pl.BlockSpec((tm, LANES), pos),
        ] + cast_in + fold_in,
        out_specs=[
            pl.BlockSpec((tm, N_Q_HEADS * LANES), row),
            pl.BlockSpec((tm, N_KV_HEADS * LANES), row),
            pl.BlockSpec((tm, 2 * LANES), row),
            pl.BlockSpec((tm, POOL_WIDTH), row),
            pl.BlockSpec((tm, 2 * D_MODEL), row),
        ] + cast_out + fold_out,
        out_shape=[
            jax.ShapeDtypeStruct((rows, N_Q_HEADS * LANES), BF16),
            jax.ShapeDtypeStruct((rows, N_KV_HEADS * LANES), BF16),
            jax.ShapeDtypeStruct((rows, 2 * LANES), BF16),
            jax.ShapeDtypeStruct((rows, POOL_WIDTH), F32),
            jax.ShapeDtypeStruct((rows, 2 * D_MODEL), BF16),
        ] + cast_shapes + fold_shape,
        compiler_params=pltpu.CompilerParams(
            dimension_semantics=("arbitrary",), vmem_limit_bytes=VMEM_LIMIT),
        name="in_proj",
    )(x2d, g_mix, w_in, b_gate, gq_t, gk_t, qshift, cos_t, shi_t, slo_t, *to_cast,
      pw2d, ps3d, w_pool_up)


def _pooled(i, n_tiles, u_ref, uprev_ref, unext_ref, icnt_ref, slab_ref, wa_ref, wb_ref):
    tq = u_ref.shape[0]
    span = tq + 2 * POOL_HALO
    zero_halo = jnp.zeros((POOL_HALO, POOL_WIDTH), F32)
    slab_ref[0:POOL_HALO, :] = jnp.where(i > 0, uprev_ref[...], zero_halo)
    slab_ref[POOL_HALO:POOL_HALO + tq, :] = u_ref[...]
    slab_ref[POOL_HALO + tq:span, :] = jnp.where(i < n_tiles - 1, unext_ref[...], zero_halo)
    slab_ref[span:, :] = zero_halo
    wa_ref[span:, :] = jnp.zeros((POOL_HALO, 2 * POOL_GROUP), F32)
    wb_ref[span:, :] = jnp.zeros((POOL_HALO, 2 * POOL_GROUP), F32)

    g0 = slice(0, POOL_GROUP)
    g1 = slice(POOL_GROUP, 2 * POOL_GROUP)
    wide = slice(2 * POOL_GROUP, POOL_WIDTH)
    h = POOL_HALO
    win2 = slab_ref[h - 1:h - 1 + tq, g0] + slab_ref[h:h + tq, g0]
    win4 = ((slab_ref[h - 2:h - 2 + tq, g1] + slab_ref[h - 1:h - 1 + tq, g1])
            + (slab_ref[h:h + tq, g1] + slab_ref[h + 1:h + 1 + tq, g1]))
    wa_ref[0:span, :] = slab_ref[0:span, wide] + slab_ref[1:span + 1, wide]
    wb_ref[0:span, :] = wa_ref[0:span, :] + wa_ref[2:span + 2, :]
    wa_ref[0:span, :] = wb_ref[0:span, :] + wb_ref[4:span + 4, :]
    win8 = wa_ref[h - 4:h - 4 + tq, 0:POOL_GROUP]
    win16 = wa_ref[0:tq, POOL_GROUP:] + wa_ref[h:h + tq, POOL_GROUP:]

    for gi, win in enumerate((win2, win4, win8, win16)):
        cols = slice(gi * POOL_GROUP, (gi + 1) * POOL_GROUP)
        slab_ref[h:h + tq, cols] = win * icnt_ref[:, cols] - u_ref[:, cols]
    return slab_ref[h:h + tq, :].astype(BF16)


def _mixer_kernel(q_ref, k_ref, v_ref, u_ref, uprev_ref, unext_ref, icnt_ref, gate_ref, x_ref,
                  wau_ref, wf_ref, wo_ref, *rest, tq, seq, subtract_max, n_cast):
    cast_src, h_ref, rest = rest[:n_cast], rest[n_cast], rest[n_cast + 1:]
    cast_dst, (slab_ref, wa_ref, wb_ref, attn_ref) = rest[:n_cast], rest[n_cast:]
    _cast_blocks(cast_src, cast_dst)
    step = pl.program_id(0)
    n_tiles = seq // tq
    i = jnp.maximum(step - 1, 0) % n_tiles

    @pl.when(step == 0)
    def _():
        attn_ref[...] = jnp.zeros_like(attn_ref)

    o_prev = attn_ref[...]

    pooled = _pooled(i, n_tiles, u_ref, uprev_ref, unext_ref, icnt_ref, slab_ref, wa_ref, wb_ref)

    v = v_ref[...]
    lane = lax.broadcasted_iota(jnp.int32, (tq, LANES), 1)
    first = lane < HEAD_DIM
    heads = []
    for h in range(N_Q_HEADS):
        g = h // (N_Q_HEADS // N_KV_HEADS)
        s = lax.dot_general(q_ref[:, h * LANES:(h + 1) * LANES], k_ref[:, g * LANES:(g + 1) * LANES],
                            (((1,), (1,)), ((), ())), preferred_element_type=F32)
        if subtract_max:
            s = s - jnp.max(s, axis=-1, keepdims=True)
        p = jnp.exp2(s).astype(BF16)
        r = jnp.dot(p, v, preferred_element_type=F32)
        heads.append(r[:, :LANES] / r[:, LANES:LANES + 1])
    half = N_Q_HEADS // 2
    attn_ref[...] = jnp.concatenate(
        [jnp.where(first, heads[t], heads[half + t]) for t in range(half)], axis=1).astype(BF16)
    parts = 2
    rows = tq // parts
    sl = [slice(j * rows, (j + 1) * rows) for j in range(parts)]
    pooled_all = pooled
    a = [jnp.dot(o_prev[sl[j]], wau_ref[...], preferred_element_type=F32) for j in range(parts)]
    p_branch = [jnp.dot(pooled_all[sl[j]], wf_ref[...], preferred_element_type=F32)
                for j in range(parts)]
    mixed = []
    for j in range(parts):
        g_a = gate_ref[sl[j], :D_MODEL].astype(F32)
        g_p = gate_ref[sl[j], D_MODEL:].astype(F32)
        mixed.append((g_a * a[j] + g_p * p_branch[j]).astype(BF16))
    for j in range(parts):
        h_ref[sl[j], :] = x_ref[sl[j], :] + jnp.dot(mixed[j], wo_ref[...],
                                                    preferred_element_type=F32)


def _mixer(q, k, v, u, icnt, gates, x2d, wau, wf, wo, *to_cast, batch, seq, subtract_max):
    tq = 512
    n_tiles = seq // tq
    total = batch * n_tiles
    cast_in, cast_out, cast_shapes = _cast_specs(to_cast, total, lambda s: jnp.minimum(s, total - 1))
    halo_blocks = tq // POOL_HALO
    seq_halos = seq // POOL_HALO
    attn_tile = lambda s: jnp.minimum(s, total - 1)
    post_tile = lambda s: jnp.maximum(s - 1, 0)
    attn_row = lambda s: (attn_tile(s), 0)
    per_batch = lambda s: (attn_tile(s) // n_tiles, 0)
    row = lambda s: (post_tile(s), 0)
    seq_row = lambda s: (post_tile(s) % n_tiles, 0)
    const2 = lambda s: (0, 0)

    def prev(s):
        t = post_tile(s)
        return ((t // n_tiles) * seq_halos + jnp.maximum((t % n_tiles) * halo_blocks - 1, 0), 0)

    def nxt(s):
        t = post_tile(s)
        return ((t // n_tiles) * seq_halos
                + jnp.minimum((t % n_tiles + 1) * halo_blocks, seq_halos - 1), 0)

    return pl.pallas_call(
        functools.partial(_mixer_kernel, tq=tq, seq=seq, subtract_max=subtract_max,
                          n_cast=len(to_cast)),
        grid=(total + 1,),
        in_specs=[
            pl.BlockSpec((tq, N_Q_HEADS * LANES), attn_row),
            pl.BlockSpec((seq, N_KV_HEADS * LANES), per_batch),
            pl.BlockSpec((seq, 2 * LANES), per_batch),
            pl.BlockSpec((tq, POOL_WIDTH), row),
            pl.BlockSpec((POOL_HALO, POOL_WIDTH), prev),
            pl.BlockSpec((POOL_HALO, POOL_WIDTH), nxt),
            pl.BlockSpec((tq, POOL_WIDTH), seq_row),
            pl.BlockSpec((tq, 2 * D_MODEL), row),
            pl.BlockSpec((tq, D_MODEL), row),
            pl.BlockSpec((ATTN_WIDTH, D_MODEL), const2),
            pl.BlockSpec((POOL_WIDTH, D_MODEL), const2),
            pl.BlockSpec((D_MODEL, D_MODEL), const2),
        ] + cast_in,
        out_specs=[pl.BlockSpec((tq, D_MODEL), row)] + cast_out,
        out_shape=[jax.ShapeDtypeStruct(x2d.shape, F32)] + cast_shapes,
        scratch_shapes=[pltpu.VMEM((tq + 3 * POOL_HALO, POOL_WIDTH), F32),
                        pltpu.VMEM((tq + 3 * POOL_HALO, 2 * POOL_GROUP), F32),
                        pltpu.VMEM((tq + 3 * POOL_HALO, 2 * POOL_GROUP), F32),
                        pltpu.VMEM((tq, ATTN_WIDTH), BF16)],
        compiler_params=pltpu.CompilerParams(
            dimension_semantics=("arbitrary",), vmem_limit_bytes=VMEM_LIMIT),
        name="mixer_max" if subtract_max else "mixer",
    )(q, k, v, u, u, u, icnt, gates, x2d, wau, wf, wo, *to_cast)


def _cross_ffn_kernel(h_ref, xk_ref, xv_ref, gc_ref, wq_ref, wo_ref, gf_ref, w1_ref, w2_ref,
                      gfin_ref, out_ref, *, parts):
    rows = h_ref.shape[0] // parts
    sl = [slice(j * rows, (j + 1) * rows) for j in range(parts)]
    every = range(parts)

    h = [h_ref[sl[j], :] for j in every]
    n2 = [(h[j] * _rms_scale(h[j]) * gc_ref[...]).astype(BF16) for j in every]
    xq = [(jnp.dot(n2[j], wq_ref[...], preferred_element_type=F32) * (X_HEAD_DIM ** -0.5))
          .astype(BF16) for j in every]
    outs = [[] for _ in every]
    for hd in range(N_X_HEADS):
        cols = slice(hd * X_HEAD_DIM, (hd + 1) * X_HEAD_DIM)
        for j in every:
            s = lax.dot_general(xq[j][:, cols], xk_ref[:, cols], (((1,), (1,)), ((), ())),
                                preferred_element_type=F32)
            p, l = _softmax_rows(s)
            o = jnp.dot(p.astype(BF16), xv_ref[:, cols], preferred_element_type=F32)
            outs[j].append((o / l).astype(BF16))
    h = [h[j] + jnp.dot(jnp.concatenate(outs[j], axis=1), wo_ref[...], preferred_element_type=F32)
         for j in every]

    n3 = [(h[j] * _rms_scale(h[j]) * gf_ref[...]).astype(BF16) for j in every]
    chunk = 1024
    acc = h
    for c in range(D_FF // chunk):
        t = [jnp.dot(n3[j], w1_ref[:, c * chunk:(c + 1) * chunk], preferred_element_type=F32)
             for j in every]
        t = [jnp.square(jnp.maximum(t[j], 0.0)).astype(BF16) for j in every]
        acc = [acc[j] + jnp.dot(t[j], w2_ref[c * chunk:(c + 1) * chunk, :],
                                preferred_element_type=F32) for j in every]
    for j in every:
        out_ref[sl[j], :] = acc[j] * _rms_scale(acc[j]) * gfin_ref[...]


def _cross_ffn(h2d, xk, xv, g_cross, w_xq, w_xo, g_ffn, w_ff1, w_ff2, g_final, seq, n_mem):
    rows = h2d.shape[0]
    tm = 1024
    tiles_per_batch = seq // tm
    const = lambda i: (0, 0)
    row = lambda i: (i, 0)
    per_batch = lambda i: (i // tiles_per_batch, 0)
    single = pl.Buffered(1)
    return pl.pallas_call(
        functools.partial(_cross_ffn_kernel, parts=4),
        grid=(rows // tm,),
        in_specs=[
            pl.BlockSpec((tm, D_MODEL), row),
            pl.BlockSpec((n_mem, D_MODEL), per_batch),
            pl.BlockSpec((n_mem, D_MODEL), per_batch),
            pl.BlockSpec((1, D_MODEL), const),
            pl.BlockSpec((D_MODEL, D_MODEL), const, pipeline_mode=single),
            pl.BlockSpec((D_MODEL, D_MODEL), const, pipeline_mode=single),
            pl.BlockSpec((1, D_MODEL), const),
            pl.BlockSpec((D_MODEL, D_FF), const, pipeline_mode=single),
            pl.BlockSpec((D_FF, D_MODEL), const, pipeline_mode=single),
            pl.BlockSpec((1, D_MODEL), const),
        ],
        out_specs=pl.BlockSpec((tm, D_MODEL), row),
        out_shape=jax.ShapeDtypeStruct(h2d.shape, F32),
        compiler_params=pltpu.CompilerParams(
            dimension_semantics=("arbitrary",), vmem_limit_bytes=VMEM_LIMIT),
        name="cross_ffn",
    )(h2d, xk, xv, g_cross, w_xq, w_xo, g_ffn, w_ff1, w_ff2, g_final)


def _rope_tables(seq):
    t = np.arange(seq)[:, None]
    d = np.arange(LANES)[None, :] % HEAD_DIM
    freq = d % ROPE_FREQS
    upper_half = (d // ROPE_FREQS) % 2 == 1
    pos = np.where(d < HEAD_DIM // 2, t // GRID_W, t % GRID_W).astype(np.float32)
    inv = np.float32(ROPE_THETA) ** (-(2 * freq).astype(np.float32) / np.float32(2 * ROPE_FREQS))
    ang = (pos * inv).astype(np.float32)
    cos, sin = np.cos(ang).astype(np.float32), np.sin(ang).astype(np.float32)
    zero = np.float32(0.0)
    return (jnp.asarray(cos), jnp.asarray(np.where(upper_half, sin, zero)),
            jnp.asarray(np.where(upper_half, zero, -sin)))


def _pool_inverse_counts(seq):
    t = np.arange(seq)[:, None]
    w = np.repeat(np.asarray(POOL_WINDOWS), POOL_GROUP)[None, :]
    lo = np.maximum(t - w // 2, 0)
    hi = np.minimum(t + (w - w // 2), seq)
    return jnp.asarray((1.0 / (hi - lo)).astype(np.float32))


def kernel(x, mem, g_mix, w_in, b_gate, g_q, g_k, w_attn_up, pool_w, pool_scale, w_pool_up, w_out,
           g_cross, g_mem, w_xq, w_xkv, w_xo, g_ffn, w_ff1, w_ff2, g_final):
    batch, seq, d = x.shape
    n_mem = mem.shape[1]
    assert d == D_MODEL and w_in.shape[0] == 1 and seq % 1024 == 0 and (batch * n_mem) % 1024 == 0
    l = 0
    x2d = x.reshape(batch * seq, d)
    mem2d = mem.reshape(batch * n_mem, d)
    row = lambda a: a.reshape(1, -1)

    cos_t, shi_t, slo_t = _rope_tables(seq)
    two_heads = lambda g: jnp.concatenate([g, g]).reshape(1, LANES)
    gq_t = two_heads(g_q[l] * (HEAD_DIM ** -0.5 * LOG2_E))
    gk_t = two_heads(g_k[l])
    shift = 1.02 * HEAD_DIM ** 0.5 * jnp.max(jnp.abs(g_q[l])) * jnp.max(jnp.abs(g_k[l])) * LOG2_E
    fixed_shift_ok = shift <= MAX_FIXED_SHIFT * LOG2_E
    qshift = jnp.where((jnp.arange(LANES) == HEAD_DIM) & fixed_shift_ok, -shift, 0.0)
    qshift = qshift.astype(F32).reshape(1, LANES)
    xk, xv, w_in_b = _mem_kv(mem2d, row(g_mem[l]), w_xkv[l], w_in[l])
    q, k, v, u, gates, wau_b, wo_b, wf_b = _in_proj(
        x2d, row(g_mix[l]), w_in_b, row(b_gate[l]), gq_t, gk_t, qshift, cos_t, shi_t, slo_t, seq,
        w_attn_up[l], w_out[l], pool_w[l], pool_scale[l], w_pool_up[l])
    mixer_args = (q, k, v, u, _pool_inverse_counts(seq), gates, x2d, wau_b, wf_b, wo_b,
                  w_xq[l], w_xo[l], w_ff1[l], w_ff2[l])
    h1, w_xq_b, w_xo_b, w_ff1_b, w_ff2_b = lax.cond(
        fixed_shift_ok,
        lambda *a: _mixer(*a, batch=batch, seq=seq, subtract_max=False),
        lambda *a: _mixer(*a, batch=batch, seq=seq, subtract_max=True),
        *mixer_args)
    out = _cross_ffn(h1, xk, xv, row(g_cross[l]), w_xq_b, w_xo_b, row(g_ffn[l]), w_ff1_b, w_ff2_b,
                     row(g_final), seq, n_mem)
    return out.reshape(batch, seq, d)
```
